```python
import math
import jax, jax.numpy as jnp
from jax import lax
import numpy as np

D_MODEL = 1024
BATCH = 16
SEQ = 4096
DEPTH = 4

CTX_LEN = 256
GRID_W = 64
N_MIXERS = 3
E_BRANCH = 2 * D_MODEL
MLA_HEADS = D_MODEL // 64
MLA_NOPE = 128
MLA_ROPE = 64
MLA_V = 128
MLA_WIDTH = MLA_HEADS * MLA_V
MLA_Q_RANK = D_MODEL // 4
MLA_KV_RANK = D_MODEL // 8
MLA_SPLITS = (MLA_Q_RANK, MLA_Q_RANK + MLA_KV_RANK, MLA_Q_RANK + MLA_KV_RANK + MLA_ROPE)
MLA_IN = MLA_Q_RANK + MLA_KV_RANK + MLA_ROPE + MLA_WIDTH
ROPE_BASE = 10000.0
Q_BLOCK = 128
S5_GROUP = 16
S5_GROUPS = E_BRANCH // S5_GROUP
S5_STATE = 64
S5_CHUNK = 128
DT_MIN = 0.001
DT_MAX = 0.1
CONV_WIDTH = 31
CONV_PAD = CONV_WIDTH // 2
DEEPNORM_ALPHA = (2.0 * DEPTH) ** 0.25
DEEPNORM_BETA = (8.0 * DEPTH) ** -0.25
NORM_EPS = 1e-6

kernel_name = "hybrid_mla_s5_conformer_dit_block"


def _layernorm(x, g, b):
    xf = x.astype(jnp.float32)
    mu = jnp.mean(xf, axis=-1, keepdims=True)
    var = jnp.mean(jnp.square(xf - mu), axis=-1, keepdims=True)
    y = (xf - mu) * lax.rsqrt(var + NORM_EPS) * g.astype(jnp.float32) + b.astype(jnp.float32)
    return y.astype(x.dtype)


def _rmsnorm(x, g):
    xf = x.astype(jnp.float32)
    y = xf * lax.rsqrt(jnp.mean(jnp.square(xf), axis=-1, keepdims=True) + NORM_EPS) * g.astype(jnp.float32)
    return y.astype(x.dtype)


def _axial_rope_tables(rows):
    r, col = jnp.meshgrid(jnp.arange(rows, dtype=jnp.float32), jnp.arange(GRID_W, dtype=jnp.float32), indexing='ij')
    r, col = r.reshape(-1), col.reshape(-1)
    quarter = MLA_ROPE // 4
    inv = ROPE_BASE ** (-jnp.arange(quarter, dtype=jnp.float32) / quarter)
    ang_r = r[:, None] * inv
    ang_c = col[:, None] * inv
    ang = jnp.concatenate([ang_r, ang_r, ang_c, ang_c], axis=-1)
    return jnp.cos(ang), jnp.sin(ang)


def _apply_rope(v, cos, sin):
    x1, x2, x3, x4 = jnp.split(v, 4, axis=-1)
    rot = jnp.concatenate([-x2, x1, -x4, x3], axis=-1)
    return v * cos.astype(v.dtype) + rot * sin.astype(v.dtype)


def _mla_split(h, w_in, kv_norm):
    q_dn, kv_dn, k_rope, gate = jnp.split(h @ w_in, MLA_SPLITS, axis=-1)
    return q_dn, _rmsnorm(kv_dn, kv_norm), k_rope, gate


def _mla_queries(q_dn, q_norm, w_uq, w_uk):
    b, n, _ = q_dn.shape
    q = (_rmsnorm(q_dn, q_norm) @ w_uq).reshape(b, n, MLA_HEADS, MLA_NOPE + MLA_ROPE)
    q_abs = jnp.einsum('bnhd,chd->bnhc', q[..., :MLA_NOPE], w_uk)
    return q_abs, q[..., MLA_NOPE:]


def _mla_output(o_lat, gate, w_uv, w_out):
    b, n = o_lat.shape[:2]
    v = jnp.einsum('bnhc,chv->bnhv', o_lat, w_uv).reshape(b, n, MLA_WIDTH)
    return (v * jax.nn.silu(gate)) @ w_out


def _mla_mixer(h_lat, h_ctx, cos, sin, w_in, q_norm, kv_norm, w_uq, w_uk, w_uv, w_out, with_ctx_out):
    b, s, _ = h_lat.shape
    scale = (MLA_NOPE + MLA_ROPE) ** -0.5
    q_dn_l, ckv_l, kr_l, g_l = _mla_split(h_lat, w_in, kv_norm)
    q_dn_c, ckv_c, kr_c, g_c = _mla_split(h_ctx, w_in, kv_norm)
    qa_l, qr_l = _mla_queries(q_dn_l, q_norm, w_uq, w_uk)
    keys_lat = jnp.concatenate([ckv_l, _apply_rope(kr_l, cos, sin)], axis=-1)
    keys_ctx = jnp.concatenate([ckv_c, kr_c], axis=-1)

    def block(args):
        qa, qr, cs, sn = args
        q_pos = jnp.concatenate([qa, _apply_rope(qr, cs[:, None], sn[:, None])], axis=-1)
        q_free = jnp.concatenate([qa, qr], axis=-1)
        s_all = jnp.concatenate([jnp.einsum('bthd,bkd->bhtk', q_pos, keys_lat),
                                 jnp.einsum('bthd,bkd->bhtk', q_free, keys_ctx)], axis=-1)
        p = jax.nn.softmax(s_all.astype(jnp.float32) * scale, axis=-1).astype(ckv_l.dtype)
        return (jnp.einsum('bhtk,bkc->bthc', p[..., :s], ckv_l)
                + jnp.einsum('bhtk,bkc->bthc', p[..., s:], ckv_c))

    nblk = s // Q_BLOCK
    to_blocks = lambda a: a.reshape(b, nblk, Q_BLOCK, *a.shape[2:]).swapaxes(0, 1)
    o = lax.map(block, (to_blocks(qa_l), to_blocks(qr_l),
                        cos.reshape(nblk, Q_BLOCK, MLA_ROPE), sin.reshape(nblk, Q_BLOCK, MLA_ROPE)))
    o = o.swapaxes(0, 1).reshape(b, s, MLA_HEADS, MLA_KV_RANK)
    y_lat = _mla_output(o, g_l, w_uv, w_out)
    if not with_ctx_out:
        return y_lat, None
    qa_c, qr_c = _mla_queries(q_dn_c, q_norm, w_uq, w_uk)
    q_c = jnp.concatenate([qa_c, qr_c], axis=-1)
    p_c = jax.nn.softmax(jnp.einsum('bthd,bkd->bhtk', q_c, keys_ctx).astype(jnp.float32) * scale,
                         axis=-1).astype(ckv_c.dtype)
    o_c = jnp.einsum('bhtk,bkc->bthc', p_c, ckv_c)
    return y_lat, _mla_output(o_c, g_c, w_uv, w_out)


def _ssm_combine(left, right):
    a_l, b_l = left
    a_r, b_r = right
    return a_r * a_l, a_r * b_l + b_r


def _s5_direction(u_ctx, u_lat, lam, dt, bmat, cmat):
    lam_dt = lam * dt[:, None]
    lam_bar = jnp.exp(lam_dt)
    b_bar = ((lam_bar - 1.0) / lam)[..., None] * bmat
    powers = jnp.exp(lam_dt[None] * jnp.arange(1, S5_CHUNK + 1, dtype=jnp.float32)[:, None, None])

    def step(h, u_chunk):
        bu = jnp.einsum('btgs,gps->btgp', u_chunk.astype(jnp.complex64), b_bar)
        a = jnp.broadcast_to(lam_bar, bu.shape)
        _, xs = lax.associative_scan(_ssm_combine, (a, bu), axis=1)
        xs = xs + powers[None] * h[:, None]
        y = jnp.einsum('btgp,gsp->btgs', xs, cmat).real
        return xs[:, -1], y

    def run(u, h0):
        bsz, n = u.shape[:2]
        nch = n // S5_CHUNK
        chunks = u.reshape(bsz, nch, S5_CHUNK, S5_GROUPS, S5_GROUP).swapaxes(0, 1)
        h_fin, ys = lax.scan(step, h0, chunks)
        return h_fin, ys.swapaxes(0, 1).reshape(bsz, n, S5_GROUPS * S5_GROUP)

    h0 = jnp.zeros((u_lat.shape[0], S5_GROUPS, S5_STATE), jnp.complex64)
    h_ctx, y_ctx = run(u_ctx, h0)
    _, y_lat = run(u_lat, h_ctx)
    return y_ctx, y_lat


def _s5_mixer(h_lat, h_ctx, w_in, lam_re, lam_im, log_dt, b_re, b_im, c_re, c_im, d_skip, w_glu, b_glu, w_out,
              with_ctx_out):
    u_l, g_l = jnp.split(h_lat @ w_in, 2, axis=-1)
    u_c, g_c = jnp.split(h_ctx @ w_in, 2, axis=-1)
    ul, uc = u_l.astype(jnp.float32), u_c.astype(jnp.float32)
    dsk = d_skip.astype(jnp.float32)
    y_l, y_c = dsk * ul, dsk * uc
    f32 = lambda a: a.astype(jnp.float32)
    for direction in range(2):
        lam = lax.complex(jnp.minimum(f32(lam_re[direction]), -1e-4), f32(lam_im[direction]))
        dt = jnp.exp(f32(log_dt[direction]))
        bmat = lax.complex(f32(b_re[direction]), f32(b_im[direction]))
        cmat = lax.complex(f32(c_re[direction]), f32(c_im[direction]))
        if direction == 0:
            yc, yl = _s5_direction(uc, ul, lam, dt, bmat, cmat)
        else:
            yc, yl = _s5_direction(jnp.flip(uc, 1), jnp.flip(ul, 1), lam, dt, bmat, cmat)
            yc, yl = jnp.flip(yc, 1), jnp.flip(yl, 1)
        y_l, y_c = y_l + yl, y_c + yc

    def glu_out(y, g):
        y = jax.nn.gelu(y.astype(g.dtype))
        ya, yb = jnp.split(y @ w_glu + b_glu, 2, axis=-1)
        return ((ya * jax.nn.sigmoid(yb)) * jax.nn.silu(g)) @ w_out

    y_ctx = glu_out(y_c, g_c) if with_ctx_out else None
    return glu_out(y_l, g_l), y_ctx


def _conv_mixer(h_lat, h_ctx, w_in, dw, dw_b, ln_g, ln_b, w_out, with_ctx_out):
    def branch(h):
        a, bgate, g = jnp.split(h @ w_in, 3, axis=-1)
        v = a * jax.nn.sigmoid(bgate)
        v = lax.conv_general_dilated(v, dw[:, None, :], window_strides=(1,), padding=[(CONV_PAD, CONV_PAD)],
                                     dimension_numbers=('NWC', 'WIO', 'NWC'),
                                     feature_group_count=E_BRANCH) + dw_b
        v = jax.nn.silu(_layernorm(v, ln_g, ln_b))
        return (v * jax.nn.silu(g)) @ w_out
    y_ctx = branch(h_ctx) if with_ctx_out else None
    return branch(h_lat), y_ctx


def setup_inputs(seed: int = 0) -> dict:
    key = jax.random.key(seed)
    ks = iter(jax.random.split(key, 48))
    f = jnp.float32
    nrm = lambda shape, scale: jax.random.normal(next(ks), shape, f) * scale
    n_a, n_b, n_c = [len(range(k, DEPTH, N_MIXERS)) for k in range(N_MIXERS)]
    D, E, G, P, GS = D_MODEL, E_BRANCH, S5_GROUPS, S5_STATE, S5_GROUP
    lam_im = jnp.broadcast_to(math.pi * jnp.arange(P, dtype=f), (n_b, 2, G, P))
    return {
        'x': nrm((BATCH, SEQ, D), 1.0),
        'c': nrm((BATCH, D), 1.0),
        'ctx': nrm((BATCH, CTX_LEN, D), 1.0),
        'c_ctx': nrm((D,), 1.0),
        'w_mod': nrm((DEPTH, D, 3 * D), 0.5 * D ** -0.5),
        'b_mod': nrm((DEPTH, 3 * D), 0.02),
        'ln_g': 1.0 + nrm((DEPTH, D), 0.02),
        'ln_b': nrm((DEPTH, D), 0.02),
        'mla_w_in': nrm((n_a, D, MLA_IN), D ** -0.5),
        'mla_q_norm': 1.0 + nrm((n_a, MLA_Q_RANK), 0.02),
        'mla_kv_norm': 1.0 + nrm((n_a, MLA_KV_RANK), 0.02),
        'mla_w_uq': nrm((n_a, MLA_Q_RANK, MLA_HEADS * (MLA_NOPE + MLA_ROPE)), MLA_Q_RANK ** -0.5),
        'mla_w_uk': nrm((n_a, MLA_KV_RANK, MLA_HEADS, MLA_NOPE), MLA_NOPE ** -0.5),
        'mla_w_uv': nrm((n_a, MLA_KV_RANK, MLA_HEADS, MLA_V), MLA_KV_RANK ** -0.5),
        'mla_w_out': nrm((n_a, MLA_WIDTH, D), DEEPNORM_BETA * MLA_WIDTH ** -0.5),
        's5_w_in': nrm((n_b, D, 2 * E), D ** -0.5),
        's5_lam_re': -0.5 + nrm((n_b, 2, G, P), 0.01),
        's5_lam_im': lam_im,
        's5_log_dt': jax.random.uniform(next(ks), (n_b, 2, G), f, math.log(DT_MIN), math.log(DT_MAX)),
        's5_b_re': nrm((n_b, 2, G, P, GS), (2.0 * GS) ** -0.5),
        's5_b_im': nrm((n_b, 2, G, P, GS), (2.0 * GS) ** -0.5),
        's5_c_re': nrm((n_b, 2, G, GS, P), P ** -0.5),
        's5_c_im': nrm((n_b, 2, G, GS, P), P ** -0.5),
        's5_d': nrm((n_b, E), 0.5),
        's5_w_glu': nrm((n_b, E, 2 * E), E ** -0.5),
        's5_b_glu': nrm((n_b, 2 * E), 0.02),
        's5_w_out': nrm((n_b, E, D), DEEPNORM_BETA * E ** -0.5),
        'cv_w_in': nrm((n_c, D, 3 * E), D ** -0.5),
        'cv_dw': nrm((n_c, CONV_WIDTH, E), CONV_WIDTH ** -0.5),
        'cv_dw_b': nrm((n_c, E), 0.02),
        'cv_ln_g': 1.0 + nrm((n_c, E), 0.02),
        'cv_ln_b': nrm((n_c, E), 0.02),
        'cv_w_out': nrm((n_c, E, D), DEEPNORM_BETA * E ** -0.5),
    }


def reference(x, c, ctx, c_ctx, w_mod, b_mod, ln_g, ln_b,
              mla_w_in, mla_q_norm, mla_kv_norm, mla_w_uq, mla_w_uk, mla_w_uv, mla_w_out,
              s5_w_in, s5_lam_re, s5_lam_im, s5_log_dt, s5_b_re, s5_b_im, s5_c_re, s5_c_im, s5_d,
              s5_w_glu, s5_b_glu, s5_w_out,
              cv_w_in, cv_dw, cv_dw_b, cv_ln_g, cv_ln_b, cv_w_out):
    n_tokens = x.shape[1]
    ROWS = n_tokens // GRID_W
    cos, sin = _axial_rope_tables(ROWS)
    cond = jax.nn.silu(c)
    cond_ctx = jax.nn.silu(c_ctx)
    for i in range(DEPTH):
        kind, j = i % N_MIXERS, i // N_MIXERS
        with_ctx_out = i < DEPTH - 1
        sh, sc, gt = jnp.split(cond @ w_mod[i] + b_mod[i], 3, axis=-1)
        sh_c, sc_c, gt_c = jnp.split(cond_ctx @ w_mod[i] + b_mod[i], 3, axis=-1)
        h_lat = x * (1.0 + sc[:, None]) + sh[:, None]
        h_ctx = ctx * (1.0 + sc_c) + sh_c
        if kind == 0:
            y_lat, y_ctx = _mla_mixer(h_lat, h_ctx, cos, sin, mla_w_in[j], mla_q_norm[j], mla_kv_norm[j],
                                      mla_w_uq[j], mla_w_uk[j], mla_w_uv[j], mla_w_out[j], with_ctx_out)
        elif kind == 1:
            y_lat, y_ctx = _s5_mixer(h_lat, h_ctx, s5_w_in[j], s5_lam_re[j], s5_lam_im[j], s5_log_dt[j],
                                     s5_b_re[j], s5_b_im[j], s5_c_re[j], s5_c_im[j], s5_d[j],
                                     s5_w_glu[j], s5_b_glu[j], s5_w_out[j], with_ctx_out)
        else:
            y_lat, y_ctx = _conv_mixer(h_lat, h_ctx, cv_w_in[j], cv_dw[j], cv_dw_b[j], cv_ln_g[j], cv_ln_b[j],
                                       cv_w_out[j], with_ctx_out)
        x = _layernorm(DEEPNORM_ALPHA * x + gt[:, None] * y_lat, ln_g[i], ln_b[i])
        if with_ctx_out:
            ctx = _layernorm(DEEPNORM_ALPHA * ctx + gt_c * y_ctx, ln_g[i], ln_b[i])
    return x
```

```python
import functools
import math

import jax
import jax.numpy as jnp
from jax import lax
from jax.experimental import pallas as pl
from jax.experimental.pallas import tpu as pltpu

F32 = jnp.float32
BF16 = jnp.bfloat16

D_MODEL = 1024
DEPTH = 4
N_MIXERS = 3
GRID_W = 64
E_BRANCH = 2 * D_MODEL
MLA_HEADS = 16
MLA_NOPE = 128
MLA_ROPE = 64
MLA_V = 128
MLA_Q_RANK = 256
MLA_KV_RANK = 128
MLA_KEY = 256
MLA_QCOLS = 384
ROPE_BASE = 10000.0
S5_GROUP = 16
S5_GROUPS = E_BRANCH // S5_GROUP
S5_STATE = 64
S5_CHUNK = 128
CONV_WIDTH = 31
CONV_PAD = CONV_WIDTH // 2
CONV_HALO = 16
DEEPNORM_ALPHA = (2.0 * DEPTH) ** 0.25
NORM_EPS = 1e-6
LANE = 128
VMEM_LIMIT = 56 * 1024 * 1024

TOK_TILE = 256
ATT_TQ = 128
MOD_ROWS = 24


def _sigmoid(x):
    return 1.0 / (1.0 + jnp.exp(-x))


def _silu(x):
    return x * _sigmoid(x)


def _gelu_tanh(x):
    return 0.5 * x * (1.0 + jnp.tanh(math.sqrt(2.0 / math.pi) * (x + 0.044715 * (x * x * x))))


def _layernorm(r, g, b):
    mu = jnp.mean(r, axis=-1, keepdims=True)
    d = r - mu
    var = jnp.mean(d * d, axis=-1, keepdims=True)
    return d * lax.rsqrt(var + NORM_EPS) * g + b


def _rmsnorm(x, g):
    return x * lax.rsqrt(jnp.mean(x * x, axis=-1, keepdims=True) + NORM_EPS) * g


def _params(sem):
    return pltpu.CompilerParams(dimension_semantics=sem, vmem_limit_bytes=VMEM_LIMIT)


def _const_spec(shape):
    nd = len(shape)
    return pl.BlockSpec(shape, lambda *_: (0,) * nd, pipeline_mode=pl.Buffered(1))


def _mod_kernel(c_ref, w_ref, b_ref, o_ref):
    c = c_ref[...]
    o_ref[0] = jnp.dot(_silu(c), w_ref[0], preferred_element_type=F32,
                       precision=lax.Precision.HIGHEST) + b_ref[0]


def _modulation(c, c_ctx, w_mod, b_mod):
    bsz, d = c.shape
    assert bsz + 1 <= MOD_ROWS
    rows = jnp.concatenate([c, c_ctx[None], jnp.zeros((MOD_ROWS - bsz - 1, d), F32)], axis=0)
    out = pl.pallas_call(
        _mod_kernel,
        grid=(DEPTH, 3),
        in_specs=[pl.BlockSpec((MOD_ROWS, d), lambda i, j: (0, 0)),
                  pl.BlockSpec((1, d, d), lambda i, j: (i, 0, j)),
                  pl.BlockSpec((1, 1, d), lambda i, j: (i, 0, j))],
        out_specs=pl.BlockSpec((1, MOD_ROWS, d), lambda i, j: (i, 0, j)),
        out_shape=jax.ShapeDtypeStruct((DEPTH, MOD_ROWS, 3 * d), F32),
        compiler_params=_params(("parallel", "parallel")),
        name="modulation",
    )(rows, w_mod, b_mod.reshape(DEPTH, 1, 3 * d))
    out = out.reshape(DEPTH, MOD_ROWS, 3, d)
    lat = out[:, :bsz]
    ctx = jnp.broadcast_to(out[:, bsz][:, None], lat.shape)
    return jnp.stack([ctx, lat], axis=2)


def _mod_spec(nct):
    return pl.BlockSpec((1, 1, 3, D_MODEL), lambda b, t: (b, jnp.where(t < nct, 0, 1), 0, 0))


def _modulate(x_ref, mod_ref):
    return x_ref[0] * (1.0 + mod_ref[0, 0, 1:2, :]) + mod_ref[0, 0, 0:1, :]


def _residual_norm(x_ref, mod_ref, y, lng_ref, lnb_ref):
    r = DEEPNORM_ALPHA * x_ref[0] + mod_ref[0, 0, 2:3, :] * y
    return _layernorm(r, lng_ref[...], lnb_ref[...])


def _rot_cols(w):
    x1, x2, x3, x4 = jnp.split(w, 4, axis=-1)
    return jnp.concatenate([-x2, x1, -x4, x3], axis=-1)


def _mla_in_kernel(x_ref, mod_ref, cq_ref, sq_ref, ck_ref, w1_ref, w2_ref, wuk_ref, qn_ref, kvn_ref,
                   q_ref, k_ref, g_ref, *, tq, qscale):
    tm = x_ref.shape[1]
    h = _modulate(x_ref, mod_ref).astype(BF16)
    z = jnp.dot(h, w1_ref[...], preferred_element_type=F32)
    qn = _rmsnorm(z[:, 0:MLA_Q_RANK], qn_ref[...]).astype(BF16)
    ckv = _rmsnorm(z[:, 256:384], kvn_ref[...])
    sq = sq_ref[...]
    kk = z[:, 384:512] * ck_ref[...] + z[:, 512:640] * sq
    k_ref[0, :, 0:LANE] = ckv.astype(BF16)
    k_ref[0, :, LANE:2 * LANE] = kk.astype(BF16)
    g_ref[0] = z[:, 640:].astype(BF16)
    qq = jnp.dot(qn, w2_ref[...], preferred_element_type=F32)
    cq = cq_ref[...]
    for hh in range(MLA_HEADS):
        base = hh * MLA_QCOLS
        nope = qq[:, base:base + LANE].astype(BF16)
        qa = jnp.dot(nope, wuk_ref[hh], preferred_element_type=F32) * qscale
        qr = (qq[:, base + LANE:base + 2 * LANE] * cq + qq[:, base + 2 * LANE:base + 3 * LANE] * sq) * qscale
        for s in range(tm // tq):
            q_ref[0, s, hh * tq:(hh + 1) * tq, 0:LANE] = qa[s * tq:(s + 1) * tq].astype(BF16)
            q_ref[0, s, hh * tq:(hh + 1) * tq, LANE:2 * LANE] = qr[s * tq:(s + 1) * tq].astype(BF16)


def _attn_kernel(q_ref, k_ref, o_ref, m_sc, l_sc, acc_sc, *, tq, tk, q_off, nct_q, n_ctx_chunks, n_all_chunks):
    t = pl.program_id(1) + q_off
    nchunks = jnp.where(t < nct_q, n_ctx_chunks, n_all_chunks)
    m_sc[...] = jnp.full(m_sc.shape, -1e30, F32)
    l_sc[...] = jnp.zeros(l_sc.shape, F32)
    acc_sc[...] = jnp.zeros(acc_sc.shape, F32)
    q = q_ref[0, 0]

    def body(c, carry):
        off = pl.multiple_of(c * tk, tk)
        kc = k_ref[0, pl.ds(off, tk), :]
        s = lax.dot_general(q, kc, (((1,), (1,)), ((), ())), preferred_element_type=F32)
        m_prev = m_sc[...]
        m_new = jnp.maximum(m_prev, jnp.max(s, axis=-1, keepdims=True))
        alpha = jnp.exp2(m_prev - m_new)
        p = jnp.concatenate([jnp.exp2(s[:, j * LANE:(j + 1) * LANE] - m_new) for j in range(tk // LANE)], axis=1)
        l_sc[...] = alpha * l_sc[...] + jnp.sum(p, axis=-1, keepdims=True)
        acc_sc[...] = alpha * acc_sc[...] + jnp.dot(p.astype(BF16), kc[:, 0:MLA_KV_RANK],
                                                    preferred_element_type=F32)
        m_sc[...] = m_new
        return carry

    lax.fori_loop(0, nchunks, body, 0)
    o = acc_sc[...] / l_sc[...]
    for hh in range(MLA_HEADS):
        o_ref[0, :, hh * LANE:(hh + 1) * LANE] = o[hh * tq:(hh + 1) * tq].astype(BF16)


def _mla_out_kernel(o_ref, g_ref, x_ref, mod_ref, wuv_ref, wout_ref, lng_ref, lnb_ref, out_ref):
    parts = []
    for hh in range(MLA_HEADS):
        v = jnp.dot(o_ref[0, :, hh * LANE:(hh + 1) * LANE], wuv_ref[hh], preferred_element_type=F32)
        parts.append((v * _silu(g_ref[0, :, hh * LANE:(hh + 1) * LANE].astype(F32))).astype(BF16))
    vg = jnp.concatenate(parts, axis=1)
    y = jnp.dot(vg, wout_ref[...], preferred_element_type=F32)
    out_ref[0] = _residual_norm(x_ref, mod_ref, y, lng_ref, lnb_ref)


def _mla_layer(xt, mod, tables, n_ctx, w_in, q_norm, kv_norm, w_uq, w_uk, w_uv, w_out, ln_g, ln_b, with_ctx_out):
    bsz, nt, d = xt.shape
    tm, tq = TOK_TILE, ATT_TQ
    tk = min(256, n_ctx)
    assert n_ctx % tm == 0 and nt % tm == 0 and n_ctx % tk == 0 and nt % tk == 0
    nct = n_ctx // tm
    ntile = nt // tm
    hdim = MLA_HEADS
    qscale = (MLA_NOPE + MLA_ROPE) ** -0.5 * math.log2(math.e)

    kr = w_in[:, 384:448]
    w1 = jnp.concatenate([w_in[:, 0:384], kr, kr, _rot_cols(kr), jnp.zeros_like(kr), w_in[:, 448:]],
                         axis=1).astype(BF16)
    uq = w_uq.reshape(MLA_Q_RANK, hdim, MLA_NOPE + MLA_ROPE)
    rp = uq[:, :, MLA_NOPE:]
    w2 = jnp.concatenate([uq[:, :, :MLA_NOPE], rp, rp, _rot_cols(rp), jnp.zeros_like(rp)],
                         axis=2).reshape(MLA_Q_RANK, hdim * MLA_QCOLS).astype(BF16)
    wuk = jnp.transpose(w_uk, (1, 2, 0)).astype(BF16)
    wuv = jnp.transpose(w_uv, (1, 0, 2)).astype(BF16)
    cq, sq, ck = tables

    tok = lambda last: pl.BlockSpec((1, tm, last), lambda b, t: (b, t, 0))
    tab = pl.BlockSpec((tm, LANE), lambda b, t: (t, 0))
    q, k, g = pl.pallas_call(
        functools.partial(_mla_in_kernel, tq=tq, qscale=qscale),
        grid=(bsz, ntile),
        in_specs=[tok(d), _mod_spec(nct), tab, tab, tab,
                  _const_spec(w1.shape), _const_spec(w2.shape), _const_spec(wuk.shape),
                  _const_spec((1, MLA_Q_RANK)), _const_spec((1, MLA_KV_RANK))],
        out_specs=[pl.BlockSpec((1, tm // tq, hdim * tq, MLA_KEY), lambda b, t: (b, t, 0, 0)),
                   tok(MLA_KEY), tok(hdim * MLA_V)],
        out_shape=[jax.ShapeDtypeStruct((bsz, nt // tq, hdim * tq, MLA_KEY), BF16),
                   jax.ShapeDtypeStruct((bsz, nt, MLA_KEY), BF16),
                   jax.ShapeDtypeStruct((bsz, nt, hdim * MLA_V), BF16)],
        compiler_params=_params(("parallel", "parallel")),
        name="mla_in",
    )(xt, mod, cq, sq, ck, w1, w2, wuk, q_norm.reshape(1, -1), kv_norm.reshape(1, -1))

    q_off = 0 if with_ctx_out else n_ctx // tq
    nq = nt // tq - q_off
    rows = hdim * tq
    o = pl.pallas_call(
        functools.partial(_attn_kernel, tq=tq, tk=tk, q_off=q_off, nct_q=n_ctx // tq,
                          n_ctx_chunks=n_ctx // tk, n_all_chunks=nt // tk),
        grid=(bsz, nq),
        in_specs=[pl.BlockSpec((1, 1, rows, MLA_KEY), lambda b, t: (b, t + q_off, 0, 0)),
                  pl.BlockSpec((1, nt, MLA_KEY), lambda b, t: (b, 0, 0))],
        out_specs=pl.BlockSpec((1, tq, hdim * MLA_V), lambda b, t: (b, t, 0)),
        out_shape=jax.ShapeDtypeStruct((bsz, nq * tq, hdim * MLA_V), BF16),
        scratch_shapes=[pltpu.VMEM((rows, LANE), F32), pltpu.VMEM((rows, LANE), F32),
                        pltpu.VMEM((rows, MLA_KV_RANK), F32)],
        compiler_params=_params(("parallel", "parallel")),
        name="mla_attention",
    )(q, k)

    t_off = 0 if with_ctx_out else nct
    n_out = ntile - t_off
    sh = lambda last: pl.BlockSpec((1, tm, last), lambda b, t: (b, t + t_off, 0))
    return pl.pallas_call(
        _mla_out_kernel,
        grid=(bsz, n_out),
        in_specs=[tok(hdim * MLA_V), sh(hdim * MLA_V), sh(d),
                  pl.BlockSpec((1, 1, 3, d), lambda b, t: (b, jnp.where(t + t_off < nct, 0, 1), 0, 0)),
                  _const_spec(wuv.shape), _const_spec((hdim * MLA_V, d)),
                  _const_spec((1, d)), _const_spec((1, d))],
        out_specs=tok(d),
        out_shape=jax.ShapeDtypeStruct((bsz, n_out * tm, d), F32),
        compiler_params=_params(("parallel", "parallel")),
        name="mla_out",
    )(o, g, xt, mod, wuv, w_out.astype(BF16), ln_g.reshape(1, d), ln_b.reshape(1, d))


def _rope_tables(n_ctx, n_lat):
    rows = n_lat // GRID_W
    r, col = jnp.meshgrid(jnp.arange(rows, dtype=F32), jnp.arange(GRID_W, dtype=F32), indexing="ij")
    quarter = MLA_ROPE // 4
    inv = ROPE_BASE ** (-jnp.arange(quarter, dtype=F32) / quarter)
    ang_r = r.reshape(-1)[:, None] * inv
    ang_c = col.reshape(-1)[:, None] * inv
    ang = jnp.concatenate([ang_r, ang_r, ang_c, ang_c], axis=-1)
    cos, sin = jnp.cos(ang), jnp.sin(ang)
    one, zero = jnp.ones_like(cos), jnp.zeros_like(cos)
    czero, cone = jnp.zeros((n_ctx, MLA_ROPE), F32), jnp.ones((n_ctx, MLA_ROPE), F32)
    cq = jnp.concatenate([jnp.concatenate([czero, cone], 1), jnp.concatenate([cos, one], 1)], 0)
    sq = jnp.concatenate([jnp.concatenate([czero, czero], 1), jnp.concatenate([sin, zero], 1)], 0)
    ck = jnp.concatenate([jnp.concatenate([czero, cone], 1), jnp.concatenate([cos, zero], 1)], 0)
    return cq, sq, ck


def _s5_in_kernel(x_ref, mod_ref, wut_ref, wg_ref, ut_ref, g_ref):
    h = _modulate(x_ref, mod_ref).astype(BF16)
    g_ref[0] = jnp.dot(h, wg_ref[...], preferred_element_type=F32).astype(BF16)
    ut = lax.dot_general(wut_ref[...], h, (((1,), (1,)), ((), ())), preferred_element_type=F32)
    for s in range(ut_ref.shape[0]):
        ut_ref[s, 0] = ut[:, s * S5_CHUNK:(s + 1) * S5_CHUNK]


def _cexp(scale, ar, ai):
    mag = jnp.exp(scale * ar)
    ang = scale * ai
    return mag * jnp.cos(ang), mag * jnp.sin(ang)


def _s5_scan_kernel(u_ref, lre_r, lim_r, ldt_r, lre_c, lim_c, ldt_c, btr_ref, bti_ref, btilr_ref, btili_ref,
                    crepr_ref, crepi_ref, ctr_ref, cti_ref, dterm_ref, y_ref,
                    lhs_sc, wst_sc, wout_sc, wtoep_sc, v_sc, s_sc, hf_sc, hb_sc, *, nb, order_bwd):
    t_len = S5_CHUNK
    half = S5_STATE
    hi = lax.Precision.HIGHEST
    n_chunks = len(order_bwd)

    ar = jnp.minimum(lre_r[0], -1e-4) * jnp.exp(ldt_r[0])
    ai = lim_r[0] * jnp.exp(ldt_r[0])
    lr, li = jnp.minimum(lre_r[0], -1e-4), lim_r[0]
    lbr, lbi = _cexp(1.0, ar, ai)
    den = lr * lr + li * li
    fr = ((lbr - 1.0) * lr + lbi * li) / den
    fi = (lbi * lr - (lbr - 1.0) * li) / den
    tr, ti = _cexp(float(t_len), ar, ai)

    row = lax.broadcasted_iota(jnp.int32, (t_len, LANE), 0)
    lane = lax.broadcasted_iota(jnp.int32, (t_len, LANE), 1)
    e_s = jnp.where(lane < half, t_len - 1 - row, row).astype(F32)
    es_r, es_i = _cexp(e_s, ar, ai)
    bbr = btr_ref[0] * fr - bti_ref[0] * fi
    bbi = btr_ref[0] * fi + bti_ref[0] * fr
    for i in range(S5_GROUP):
        br, bi = bbr[i:i + 1], bbi[i:i + 1]
        wst_sc[i * t_len:(i + 1) * t_len, 0:LANE] = (es_r * br - es_i * bi).astype(BF16)
        wst_sc[i * t_len:(i + 1) * t_len, LANE:2 * LANE] = (es_r * bi + es_i * br).astype(BF16)

    dt_c = jnp.exp(ldt_c[0])
    ar_c = jnp.minimum(lre_c[0], -1e-4) * dt_c
    ai_c = lim_c[0] * dt_c
    is_f = row < half
    e_k = jnp.where(is_f, lane, t_len - lane).astype(F32)
    pk_r, pk_i = _cexp(e_k, ar_c, ai_c)
    lbr_c, lbi_c = _cexp(1.0, ar_c, ai_c)
    po_r = jnp.where(is_f, pk_r * lbr_c - pk_i * lbi_c, pk_r)
    po_i = jnp.where(is_f, pk_r * lbi_c + pk_i * lbr_c, pk_i)
    ctr, cti = ctr_ref[0], cti_ref[0]
    for o in range(S5_GROUP):
        cr, ci = ctr[:, o:o + 1], cti[:, o:o + 1]
        wout_sc[0:LANE, o * t_len:(o + 1) * t_len] = (cr * po_r - ci * po_i).astype(BF16)
        wout_sc[LANE:2 * LANE, o * t_len:(o + 1) * t_len] = (-(cr * po_i + ci * po_r)).astype(BF16)

    mbr = btilr_ref[0] * fr - btili_ref[0] * fi
    mbi = btilr_ref[0] * fi + btili_ref[0] * fr
    m_r = crepr_ref[0] * mbr - crepi_ref[0] * mbi
    m_i = crepr_ref[0] * mbi + crepi_ref[0] * mbr
    zero = jnp.zeros_like(pk_r)
    pcat_r = jnp.concatenate([jnp.where(is_f, zero, pk_r), jnp.where(is_f, pk_r, zero)], axis=1)
    pcat_i = jnp.concatenate([jnp.where(is_f, zero, pk_i), jnp.where(is_f, pk_i, zero)], axis=1)
    v = (jnp.dot(m_r, pcat_r, preferred_element_type=F32, precision=hi)
         - jnp.dot(m_i, pcat_i, preferred_element_type=F32, precision=hi))
    lane_m = lax.broadcasted_iota(jnp.int32, m_r.shape, 1)
    kb0 = jnp.sum(jnp.where(lane_m >= half, m_r, 0.0), axis=1, keepdims=True)
    lane_v = lax.broadcasted_iota(jnp.int32, v.shape, 1)
    v_sc[...] = v + jnp.where(lane_v == t_len, kb0 + dterm_ref[0], 0.0)

    def toep_o(o, carry):
        def toep_i(i, carry2):
            vrow = v_sc[pl.ds(o * S5_GROUP + i, 1), :]
            rolled = pltpu.roll(jnp.broadcast_to(vrow, (t_len, 2 * t_len)), 0, 1, stride=1, stride_axis=0)
            wtoep_sc[pl.ds(pl.multiple_of(i * t_len, t_len), t_len),
                     pl.ds(pl.multiple_of(o * t_len, t_len), t_len)] = rolled[:, t_len:].astype(BF16)
            return carry2
        return lax.fori_loop(0, S5_GROUP, toep_i, carry)
    lax.fori_loop(0, S5_GROUP, toep_o, 0)

    for i in range(S5_GROUP):
        lhs_sc[:, i * t_len:(i + 1) * t_len] = u_ref[:, i, :].astype(BF16)
    lhs = lhs_sc[...]
    s_sc[...] = jnp.dot(lhs, wst_sc[...], preferred_element_type=F32)

    lane_h = lax.broadcasted_iota(jnp.int32, (nb, LANE), 1)
    f_lane = lane_h < half
    h_r = jnp.zeros((nb, LANE), F32)
    h_i = jnp.zeros((nb, LANE), F32)
    for j in range(n_chunks):
        cf, cb = j, order_bwd[j]
        hf_sc[cf * nb:(cf + 1) * nb, 0:LANE] = h_r
        hf_sc[cf * nb:(cf + 1) * nb, LANE:2 * LANE] = h_i
        hb_sc[cb * nb:(cb + 1) * nb, 0:LANE] = h_r
        hb_sc[cb * nb:(cb + 1) * nb, LANE:2 * LANE] = h_i
        s_r = jnp.where(f_lane, s_sc[cf * nb:(cf + 1) * nb, 0:LANE], s_sc[cb * nb:(cb + 1) * nb, 0:LANE])
        s_i = jnp.where(f_lane, s_sc[cf * nb:(cf + 1) * nb, LANE:2 * LANE],
                        s_sc[cb * nb:(cb + 1) * nb, LANE:2 * LANE])
        h_r, h_i = tr * h_r - ti * h_i + s_r, tr * h_i + ti * h_r + s_i
    lane_2 = lax.broadcasted_iota(jnp.int32, hf_sc.shape, 1)
    h_in = jnp.where((lane_2 % LANE) < half, hf_sc[...], hb_sc[...])
    h_hi = h_in.astype(BF16)
    h_lo = (h_in - h_hi.astype(F32)).astype(BF16)
    wout = wout_sc[...]
    y = (jnp.dot(lhs, wtoep_sc[...], preferred_element_type=F32)
         + jnp.dot(h_hi, wout, preferred_element_type=F32)
         + jnp.dot(h_lo, wout, preferred_element_type=F32))
    for o in range(S5_GROUP):
        y_ref[:, o, :] = y[:, o * t_len:(o + 1) * t_len]


def _s5_out_kernel(yt_ref, g_ref, x_ref, mod_ref, wglu_ref, bglu_ref, wout_ref, lng_ref, lnb_ref, out_ref):
    parts = [jnp.transpose(_gelu_tanh(yt_ref[s, 0])).astype(BF16) for s in range(yt_ref.shape[0])]
    a = jnp.concatenate(parts, axis=0)
    z = jnp.dot(a, wglu_ref[...], preferred_element_type=F32) + bglu_ref[...]
    t = z[:, :E_BRANCH] * _sigmoid(z[:, E_BRANCH:]) * _silu(g_ref[0].astype(F32))
    y = jnp.dot(t.astype(BF16), wout_ref[...], preferred_element_type=F32)
    out_ref[0] = _residual_norm(x_ref, mod_ref, y, lng_ref, lnb_ref)


def _s5_layer(xt, mod, n_ctx, w_in, lam_re, lam_im, log_dt, b_re, b_im, c_re, c_im, d_skip, w_glu, b_glu, w_out,
              ln_g, ln_b):
    bsz, nt, d = xt.shape
    tm = TOK_TILE
    e, ng, gs, ns, tl = E_BRANCH, S5_GROUPS, S5_GROUP, S5_STATE, S5_CHUNK
    nct, ntile = n_ctx // tm, nt // tm
    nsub = tm // tl
    nch, nch_ctx = nt // tl, n_ctx // tl
    cb = nch * bsz
    assert n_ctx % tm == 0 and nt % tm == 0 and cb % 16 == 0 and bsz % 8 == 0

    tok = lambda last: pl.BlockSpec((1, tm, last), lambda b, t: (b, t, 0))
    ut, g = pl.pallas_call(
        _s5_in_kernel,
        grid=(bsz, ntile),
        in_specs=[tok(d), _mod_spec(nct), _const_spec((e, d)), _const_spec((d, e))],
        out_specs=[pl.BlockSpec((nsub, 1, e, tl), lambda b, t: (t, b, 0, 0)), tok(e)],
        out_shape=[jax.ShapeDtypeStruct((nch, bsz, e, tl), F32), jax.ShapeDtypeStruct((bsz, nt, e), BF16)],
        compiler_params=_params(("parallel", "parallel")),
        name="s5_in",
    )(xt, mod, jnp.transpose(w_in[:, :e]).astype(BF16), w_in[:, e:].astype(BF16))

    both = lambda a: jnp.concatenate([a[0], a[1]], axis=-1)
    lre, lim = both(lam_re), both(lam_im)
    ldt = jnp.repeat(jnp.transpose(log_dt), ns, axis=1)
    bt_r, bt_i = (both(jnp.swapaxes(a, 2, 3)) for a in (b_re, b_im))
    cr_r, cr_i = both(c_re), both(c_im)
    btil_r, btil_i = (jnp.tile(a, (1, gs, 1)) for a in (bt_r, bt_i))
    crep_r, crep_i = (jnp.repeat(a, gs, axis=1) for a in (cr_r, cr_i))
    ct_r, ct_i = (jnp.swapaxes(a, 1, 2) for a in (cr_r, cr_i))
    dterm = (d_skip.reshape(ng, gs, 1) * jnp.eye(gs, dtype=F32)).reshape(ng, gs * gs, 1)

    order_bwd = tuple(range(nch_ctx - 1, -1, -1)) + tuple(range(nch - 1, nch_ctx - 1, -1))
    row_spec = pl.BlockSpec((1, 1, 2 * ns), lambda gi: (gi, 0, 0))
    col_spec = pl.BlockSpec((1, 2 * ns, 1), lambda gi: (gi, 0, 0))
    g3 = lambda r, c: pl.BlockSpec((1, r, c), lambda gi: (gi, 0, 0))
    data_spec = pl.BlockSpec((cb, gs, tl), lambda gi: (0, gi, 0))
    yt = pl.pallas_call(
        functools.partial(_s5_scan_kernel, nb=bsz, order_bwd=order_bwd),
        grid=(ng,),
        in_specs=[data_spec, row_spec, row_spec, row_spec, col_spec, col_spec, col_spec,
                  g3(gs, 2 * ns), g3(gs, 2 * ns), g3(gs * gs, 2 * ns), g3(gs * gs, 2 * ns),
                  g3(gs * gs, 2 * ns), g3(gs * gs, 2 * ns), g3(2 * ns, gs), g3(2 * ns, gs), g3(gs * gs, 1)],
        out_specs=data_spec,
        out_shape=jax.ShapeDtypeStruct((cb, e, tl), F32),
        scratch_shapes=[pltpu.VMEM((cb, gs * tl), BF16), pltpu.VMEM((gs * tl, 4 * ns), BF16),
                        pltpu.VMEM((4 * ns, gs * tl), BF16), pltpu.VMEM((gs * tl, gs * tl), BF16),
                        pltpu.VMEM((gs * gs, 2 * tl), F32), pltpu.VMEM((cb, 4 * ns), F32),
                        pltpu.VMEM((cb, 4 * ns), F32), pltpu.VMEM((cb, 4 * ns), F32)],
        compiler_params=_params(("parallel",)),
        name="s5_scan",
    )(ut.reshape(cb, e, tl), lre[:, None, :], lim[:, None, :], ldt[:, None, :],
      lre[:, :, None], lim[:, :, None], ldt[:, :, None],
      bt_r, bt_i, btil_r, btil_i, crep_r, crep_i, ct_r, ct_i, dterm)

    return pl.pallas_call(
        _s5_out_kernel,
        grid=(bsz, ntile),
        in_specs=[pl.BlockSpec((nsub, 1, e, tl), lambda b, t: (t, b, 0, 0)), tok(e), tok(d), _mod_spec(nct),
                  _const_spec((e, 2 * e)), _const_spec((1, 2 * e)), _const_spec((e, d)),
                  _const_spec((1, d)), _const_spec((1, d))],
        out_specs=tok(d),
        out_shape=jax.ShapeDtypeStruct((bsz, nt, d), F32),
        compiler_params=_params(("parallel", "parallel")),
        name="s5_out",
    )(yt.reshape(nch, bsz, e, tl), g, xt, mod, w_glu.astype(BF16), b_glu.reshape(1, -1), w_out.astype(BF16),
      ln_g.reshape(1, d), ln_b.reshape(1, d))


def _conv_in_kernel(x_ref, mod_ref, w_ref, v_ref, g_ref):
    h = _modulate(x_ref, mod_ref).astype(BF16)
    z = jnp.dot(h, w_ref[...], preferred_element_type=F32)
    v_ref[0] = (z[:, :E_BRANCH] * _sigmoid(z[:, E_BRANCH:2 * E_BRANCH])).astype(BF16)
    g_ref[0] = z[:, 2 * E_BRANCH:].astype(BF16)


def _conv_out_kernel(v_ref, vp_ref, vn_ref, g_ref, x_ref, mod_ref, dw_ref, dwb_ref, cg_ref, cb_ref, wout_ref,
                     lng_ref, lnb_ref, out_ref, vext_sc, y_sc, *, nct, ntile, row_block):
    tm = v_ref.shape[1]
    t = pl.program_id(1)
    has_prev = jnp.logical_and(t != 0, t != nct)
    has_next = jnp.logical_and(t != nct - 1, t != ntile - 1)
    vext_sc[0:CONV_HALO] = jnp.where(has_prev, vp_ref[0].astype(F32), 0.0)
    vext_sc[CONV_HALO:CONV_HALO + tm] = v_ref[0].astype(F32)
    vext_sc[CONV_HALO + tm:2 * CONV_HALO + tm] = jnp.where(has_next, vn_ref[0].astype(F32), 0.0)

    def lane_block(cbk, carry):
        lo = pl.multiple_of(cbk * LANE, LANE)
        w = dw_ref[:, pl.ds(lo, LANE)]
        bias = dwb_ref[:, pl.ds(lo, LANE)]
        for rb in range(tm // row_block):
            acc = jnp.zeros((row_block, LANE), F32)
            for k in range(CONV_WIDTH):
                start = rb * row_block + k + CONV_HALO - CONV_PAD
                acc = acc + vext_sc[pl.ds(start, row_block), pl.ds(lo, LANE)] * w[k:k + 1]
            y_sc[rb * row_block:(rb + 1) * row_block, pl.ds(lo, LANE)] = acc + bias
        return carry
    lax.fori_loop(0, E_BRANCH // LANE, lane_block, 0)

    c = _silu(_layernorm(y_sc[...], cg_ref[...], cb_ref[...]))
    y = jnp.dot((c * _silu(g_ref[0].astype(F32))).astype(BF16), wout_ref[...], preferred_element_type=F32)
    out_ref[0] = _residual_norm(x_ref, mod_ref, y, lng_ref, lnb_ref)


def _conv_layer(xt, mod, n_ctx, w_in, dw, dw_b, cln_g, cln_b, w_out, ln_g, ln_b):
    bsz, nt, d = xt.shape
    tm, e = TOK_TILE, E_BRANCH
    nct, ntile = n_ctx // tm, nt // tm
    assert n_ctx % tm == 0 and nt % tm == 0
    tok = lambda last: pl.BlockSpec((1, tm, last), lambda b, t: (b, t, 0))
    v, g = pl.pallas_call(
        _conv_in_kernel,
        grid=(bsz, ntile),
        in_specs=[tok(d), _mod_spec(nct), _const_spec((d, 3 * e))],
        out_specs=[tok(e), tok(e)],
        out_shape=[jax.ShapeDtypeStruct((bsz, nt, e), BF16), jax.ShapeDtypeStruct((bsz, nt, e), BF16)],
        compiler_params=_params(("parallel", "parallel")),
        name="conv_in",
    )(xt, mod, w_in.astype(BF16))

    hb = tm // CONV_HALO
    nhb = nt // CONV_HALO
    dw_pad = jnp.concatenate([dw, jnp.zeros((32 - CONV_WIDTH, e), F32)], axis=0)
    return pl.pallas_call(
        functools.partial(_conv_out_kernel, nct=nct, ntile=ntile, row_block=64),
        grid=(bsz, ntile),
        in_specs=[tok(e),
                  pl.BlockSpec((1, CONV_HALO, e), lambda b, t: (b, jnp.maximum(t * hb - 1, 0), 0)),
                  pl.BlockSpec((1, CONV_HALO, e), lambda b, t: (b, jnp.minimum((t + 1) * hb, nhb - 1), 0)),
                  tok(e), tok(d), _mod_spec(nct),
                  _const_spec((32, e)), _const_spec((1, e)), _const_spec((1, e)), _const_spec((1, e)),
                  _const_spec((e, d)), _const_spec((1, d)), _const_spec((1, d))],
        out_specs=tok(d),
        out_shape=jax.ShapeDtypeStruct((bsz, nt, d), F32),
        scratch_shapes=[pltpu.VMEM((tm + 2 * CONV_HALO, e), F32), pltpu.VMEM((tm, e), F32)],
        compiler_params=_params(("parallel", "parallel")),
        name="conv_out",
    )(v, v, v, g, xt, mod, dw_pad, dw_b.reshape(1, e), cln_g.reshape(1, e), cln_b.reshape(1, e),
      w_out.astype(BF16), ln_g.reshape(1, d), ln_b.reshape(1, d))


def kernel(x, c, ctx, c_ctx, w_mod, b_mod, ln_g, ln_b, mla_w_in, mla_q_norm, mla_kv_norm, mla_w_uq, mla_w_uk, mla_w_uv, mla_w_out, s5_w_in, s5_lam_re, s5_lam_im, s5_log_dt, s5_b_re, s5_b_im, s5_c_re, s5_c_im, s5_d, s5_w_glu, s5_b_glu, s5_w_out, cv_w_in, cv_dw, cv_dw_b, cv_ln_g, cv_ln_b, cv_w_out):
    n_ctx, n_lat = ctx.shape[1], x.shape[1]
    mod = _modulation(c, c_ctx, w_mod, b_mod)
    tables = _rope_tables(n_ctx, n_lat)
    xt = jnp.concatenate([ctx, x], axis=1)
    for i in range(DEPTH):
        kind, j = i % N_MIXERS, i // N_MIXERS
        if kind == 0:
            xt = _mla_layer(xt, mod[i], tables, n_ctx, mla_w_in[j], mla_q_norm[j], mla_kv_norm[j], mla_w_uq[j],
                            mla_w_uk[j], mla_w_uv[j], mla_w_out[j], ln_g[i], ln_b[i], i < DEPTH - 1)
        elif kind == 1:
            xt = _s5_layer(xt, mod[i], n_ctx, s5_w_in[j], s5_lam_re[j], s5_lam_im[j], s5_log_dt[j], s5_b_re[j],
                           s5_b_im[j], s5_c_re[j], s5_c_im[j], s5_d[j], s5_w_glu[j], s5_b_glu[j], s5_w_out[j],
                           ln_g[i], ln_b[i])
        else:
            xt = _conv_layer(xt, mod[i], n_ctx, cv_w_in[j], cv_dw[j], cv_dw_b[j], cv_ln_g[j], cv_ln_b[j],
                             cv_w_out[j], ln_g[i], ln_b[i])
    return xt
```

```python
import functools
import math

import jax
import jax.numpy as jnp
from jax import lax
from jax.experimental import pallas as pl
from jax.experimental.pallas import tpu as pltpu

F32 = jnp.float32
BF16 = jnp.bfloat16

D_MODEL = 1024
DEPTH = 4
N_MIXERS = 3
GRID_W = 64
E_BRANCH = 2 * D_MODEL
MLA_HEADS = 16
MLA_NOPE = 128
MLA_ROPE = 64
MLA_V = 128
MLA_Q_RANK = 256
MLA_KV_RANK = 128
MLA_KEY = 256
MLA_QCOLS = 384
ROPE_BASE = 10000.0
S5_GROUP = 16
S5_GROUPS = E_BRANCH // S5_GROUP
S5_STATE = 64
S5_CHUNK = 128
CONV_WIDTH = 31
CONV_PAD = CONV_WIDTH // 2
CONV_HALO = 16
DEEPNORM_ALPHA = (2.0 * DEPTH) ** 0.25
NORM_EPS = 1e-6
LANE = 128
SUBLANE = 8
VMEM_LIMIT = 56 * 1024 * 1024

TOK_TILE = 256
ATT_TQ = 128
ATT_TK = 512
MOD_ROWS = 24


def _sigmoid(x):
    return 1.0 / (1.0 + jnp.exp(-x))


def _silu(x):
    return x * _sigmoid(x)


def _gelu_tanh(x):
    return 0.5 * x * (1.0 + jnp.tanh(math.sqrt(2.0 / math.pi) * (x + 0.044715 * (x * x * x))))


def _layernorm(r, g, b):
    mu = jnp.mean(r, axis=-1, keepdims=True)
    d = r - mu
    var = jnp.mean(d * d, axis=-1, keepdims=True)
    return d * lax.rsqrt(var + NORM_EPS) * g + b


def _rmsnorm(x, g):
    return x * lax.rsqrt(jnp.mean(x * x, axis=-1, keepdims=True) + NORM_EPS) * g


def _params(sem):
    return pltpu.CompilerParams(dimension_semantics=sem, vmem_limit_bytes=VMEM_LIMIT)


def _const_spec(shape):
    nd = len(shape)
    return pl.BlockSpec(shape, lambda *_: (0,) * nd, pipeline_mode=pl.Buffered(1))


def _mod_kernel(c_ref, w_ref, b_ref, o_ref):
    c = c_ref[...]
    o_ref[0] = jnp.dot(_silu(c), w_ref[0], preferred_element_type=F32,
                       precision=lax.Precision.HIGHEST) + b_ref[0]


def _modulation(c, c_ctx, w_mod, b_mod):
    bsz, d = c.shape
    assert bsz + 1 <= MOD_ROWS
    rows = jnp.concatenate([c, c_ctx[None], jnp.zeros((MOD_ROWS - bsz - 1, d), F32)], axis=0)
    out = pl.pallas_call(
        _mod_kernel,
        grid=(DEPTH, 3),
        in_specs=[pl.BlockSpec((MOD_ROWS, d), lambda i, j: (0, 0)),
                  pl.BlockSpec((1, d, d), lambda i, j: (i, 0, j)),
                  pl.BlockSpec((1, 1, d), lambda i, j: (i, 0, j))],
        out_specs=pl.BlockSpec((1, MOD_ROWS, d), lambda i, j: (i, 0, j)),
        out_shape=jax.ShapeDtypeStruct((DEPTH, MOD_ROWS, 3 * d), F32),
        compiler_params=_params(("parallel", "parallel")),
        name="modulation",
    )(rows, w_mod, b_mod.reshape(DEPTH, 1, 3 * d))
    out = out.reshape(DEPTH, MOD_ROWS, 3, d)
    lat = out[:, :bsz]
    ctx = jnp.broadcast_to(out[:, bsz][:, None], lat.shape)
    return jnp.stack([ctx, lat], axis=2)


def _mod_spec(nct):
    return pl.BlockSpec((1, 1, 3, D_MODEL), lambda b, t: (b, jnp.where(t < nct, 0, 1), 0, 0))


def _modulate(x_ref, mod_ref):
    return x_ref[0] * (1.0 + mod_ref[0, 0, 1:2, :]) + mod_ref[0, 0, 0:1, :]


def _residual_norm(x_ref, mod_ref, y, lng_ref, lnb_ref):
    r = DEEPNORM_ALPHA * x_ref[0] + mod_ref[0, 0, 2:3, :] * y
    return _layernorm(r, lng_ref[...], lnb_ref[...])


def _rot_cols(w):
    x1, x2, x3, x4 = jnp.split(w, 4, axis=-1)
    return jnp.concatenate([-x2, x1, -x4, x3], axis=-1)


def _mla_in_kernel(x_ref, mod_ref, cq_ref, sq_ref, ck_ref, w1_ref, w2_ref, wuk_ref, qn_ref, kvn_ref,
                   q_ref, k_ref, v_ref, g_ref, *, tq, qscale):
    tm = x_ref.shape[1]
    h = _modulate(x_ref, mod_ref).astype(BF16)
    z = jnp.dot(h, w1_ref[...], preferred_element_type=F32)
    qn = _rmsnorm(z[:, 0:MLA_Q_RANK], qn_ref[...]).astype(BF16)
    ckv = _rmsnorm(z[:, 256:384], kvn_ref[...]).astype(BF16)
    sq = sq_ref[...]
    kk = z[:, 384:512] * ck_ref[...] + z[:, 512:640] * sq
    k_ref[0, :, 0:LANE] = ckv
    k_ref[0, :, LANE:2 * LANE] = kk.astype(BF16)
    v_ref[0, :, 0:LANE] = ckv
    lane = lax.broadcasted_iota(jnp.int32, (tm, LANE), 1)
    v_ref[0, :, LANE:2 * LANE] = jnp.where(lane == 0, 1.0, 0.0).astype(BF16)
    g_ref[0] = z[:, 640:].astype(BF16)
    qq = jnp.dot(qn, w2_ref[...], preferred_element_type=F32)
    cq = cq_ref[...]
    for hh in range(MLA_HEADS):
        base = hh * MLA_QCOLS
        nope = qq[:, base:base + LANE].astype(BF16)
        qa = jnp.dot(nope, wuk_ref[hh], preferred_element_type=F32) * qscale
        qr = (qq[:, base + LANE:base + 2 * LANE] * cq + qq[:, base + 2 * LANE:base + 3 * LANE] * sq) * qscale
        for s in range(tm // tq):
            q_ref[0, s, hh * tq:(hh + 1) * tq, 0:LANE] = qa[s * tq:(s + 1) * tq].astype(BF16)
            q_ref[0, s, hh * tq:(hh + 1) * tq, LANE:2 * LANE] = qr[s * tq:(s + 1) * tq].astype(BF16)


def _attn_kernel(q_ref, k_ref, v_ref, o_ref, m_sc, acc_sc, sa_sc, sb_sc, *, tq, tk, q_off, nct_q, n_ctx, n_lat_chunks):
    m_sc[...] = jnp.full(m_sc.shape, -1e30, F32)
    acc_sc[...] = jnp.zeros(acc_sc.shape, F32)
    q = q_ref[0, 0]

    def scores(off, size):
        kc = k_ref[0, pl.ds(off, size), :]
        return lax.dot_general(q, kc, (((1,), (1,)), ((), ())), preferred_element_type=F32)

    def update(s, off, size):
        m_prev = m_sc[...]
        m_new = jnp.maximum(m_prev, jnp.max(s, axis=-1, keepdims=True))
        alpha = jnp.exp2(m_prev - m_new)
        p = jnp.concatenate([jnp.exp2(s[:, j * LANE:(j + 1) * LANE] - m_new).astype(BF16)
                             for j in range(size // LANE)], axis=1)
        pv = jnp.dot(p, v_ref[0, pl.ds(off, size), :], preferred_element_type=F32)
        acc_sc[:, 0:LANE] = alpha * acc_sc[:, 0:LANE] + pv[:, 0:LANE]
        acc_sc[:, LANE:2 * LANE] = alpha * acc_sc[:, LANE:2 * LANE] + pv[:, LANE:2 * LANE]
        m_sc[...] = m_new

    def lat_off(c):
        return pl.multiple_of(n_ctx + c * tk, LANE)

    def latent_chunks():
        n = n_lat_chunks
        if n == 1:
            update(scores(n_ctx, tk), n_ctx, tk)
            return
        sa_sc[...] = scores(n_ctx, tk)

        def pair(i, carry):
            sb_sc[...] = scores(lat_off(2 * i + 1), tk)
            update(sa_sc[...], lat_off(2 * i), tk)
            sa_sc[...] = scores(lat_off(2 * i + 2), tk)
            update(sb_sc[...], lat_off(2 * i + 1), tk)
            return carry

        npairs = n // 2
        if npairs > 1:
            lax.fori_loop(0, npairs - 1, pair, 0)
        last = 2 * (npairs - 1)
        sb_sc[...] = scores(n_ctx + (last + 1) * tk, tk)
        update(sa_sc[...], n_ctx + last * tk, tk)
        if n % 2:
            sa_sc[...] = scores(n_ctx + (last + 2) * tk, tk)
        update(sb_sc[...], n_ctx + (last + 1) * tk, tk)
        if n % 2:
            update(sa_sc[...], n_ctx + (last + 2) * tk, tk)

    update(scores(0, n_ctx), 0, n_ctx)
    if q_off >= nct_q:
        latent_chunks()
    else:
        pl.when(pl.program_id(1) + q_off >= nct_q)(latent_chunks)

    o = acc_sc[:, 0:LANE] * (1.0 / acc_sc[:, LANE:LANE + 1])
    for hh in range(MLA_HEADS):
        o_ref[0, :, hh * LANE:(hh + 1) * LANE] = o[hh * tq:(hh + 1) * tq].astype(BF16)


def _mla_out_kernel(o_ref, g_ref, x_ref, mod_ref, wuv_ref, wout_ref, lng_ref, lnb_ref, out_ref):
    parts = []
    for hh in range(MLA_HEADS):
        v = jnp.dot(o_ref[0, :, hh * LANE:(hh + 1) * LANE], wuv_ref[hh], preferred_element_type=F32)
        parts.append((v * _silu(g_ref[0, :, hh * LANE:(hh + 1) * LANE].astype(F32))).astype(BF16))
    vg = jnp.concatenate(parts, axis=1)
    y = jnp.dot(vg, wout_ref[...], preferred_element_type=F32)
    out_ref[0] = _residual_norm(x_ref, mod_ref, y, lng_ref, lnb_ref)


def _mla_layer(xt, mod, tables, n_ctx, w_in, q_norm, kv_norm, w_uq, w_uk, w_uv, w_out, ln_g, ln_b, with_ctx_out):
    bsz, nt, d = xt.shape
    tm, tq = TOK_TILE, ATT_TQ
    tk = ATT_TK
    assert n_ctx % tm == 0 and nt % tm == 0 and n_ctx <= ATT_TK and (nt - n_ctx) % tk == 0
    nct = n_ctx // tm
    ntile = nt // tm
    hdim = MLA_HEADS
    qscale = (MLA_NOPE + MLA_ROPE) ** -0.5 * math.log2(math.e)

    kr = w_in[:, 384:448]
    w1 = jnp.concatenate([w_in[:, 0:384], kr, kr, _rot_cols(kr), jnp.zeros_like(kr), w_in[:, 448:]],
                         axis=1).astype(BF16)
    uq = w_uq.reshape(MLA_Q_RANK, hdim, MLA_NOPE + MLA_ROPE)
    rp = uq[:, :, MLA_NOPE:]
    w2 = jnp.concatenate([uq[:, :, :MLA_NOPE], rp, rp, _rot_cols(rp), jnp.zeros_like(rp)],
                         axis=2).reshape(MLA_Q_RANK, hdim * MLA_QCOLS).astype(BF16)
    wuk = jnp.transpose(w_uk, (1, 2, 0)).astype(BF16)
    wuv = jnp.transpose(w_uv, (1, 0, 2)).astype(BF16)
    cq, sq, ck = tables

    tok = lambda last: pl.BlockSpec((1, tm, last), lambda b, t: (b, t, 0))
    tab = pl.BlockSpec((tm, LANE), lambda b, t: (t, 0))
    q, k, v, g = pl.pallas_call(
        functools.partial(_mla_in_kernel, tq=tq, qscale=qscale),
        grid=(bsz, ntile),
        in_specs=[tok(d), _mod_spec(nct), tab, tab, tab,
                  _const_spec(w1.shape), _const_spec(w2.shape), _const_spec(wuk.shape),
                  _const_spec((1, MLA_Q_RANK)), _const_spec((1, MLA_KV_RANK))],
        out_specs=[pl.BlockSpec((1, tm // tq, hdim * tq, MLA_KEY), lambda b, t: (b, t, 0, 0)),
                   tok(MLA_KEY), tok(MLA_KEY), tok(hdim * MLA_V)],
        out_shape=[jax.ShapeDtypeStruct((bsz, nt // tq, hdim * tq, MLA_KEY), BF16),
                   jax.ShapeDtypeStruct((bsz, nt, MLA_KEY), BF16),
                   jax.ShapeDtypeStruct((bsz, nt, MLA_KEY), BF16),
                   jax.ShapeDtypeStruct((bsz, nt, hdim * MLA_V), BF16)],
        compiler_params=_params(("parallel", "parallel")),
        name="mla_in",
    )(xt, mod, cq, sq, ck, w1, w2, wuk, q_norm.reshape(1, -1), kv_norm.reshape(1, -1))

    q_off = 0 if with_ctx_out else n_ctx // tq
    nq = nt // tq - q_off
    rows = hdim * tq
    o = pl.pallas_call(
        functools.partial(_attn_kernel, tq=tq, tk=tk, q_off=q_off, nct_q=n_ctx // tq, n_ctx=n_ctx,
                          n_lat_chunks=(nt - n_ctx) // tk),
        grid=(bsz, nq),
        in_specs=[pl.BlockSpec((1, 1, rows, MLA_KEY), lambda b, t: (b, t + q_off, 0, 0)),
                  pl.BlockSpec((1, nt, MLA_KEY), lambda b, t: (b, 0, 0)),
                  pl.BlockSpec((1, nt, MLA_KEY), lambda b, t: (b, 0, 0))],
        out_specs=pl.BlockSpec((1, tq, hdim * MLA_V), lambda b, t: (b, t, 0)),
        out_shape=jax.ShapeDtypeStruct((bsz, nq * tq, hdim * MLA_V), BF16),
        scratch_shapes=[pltpu.VMEM((rows, LANE), F32), pltpu.VMEM((rows, 2 * LANE), F32),
                        pltpu.VMEM((rows, tk), F32), pltpu.VMEM((rows, tk), F32)],
        compiler_params=_params(("parallel", "parallel")),
        name="mla_attention",
    )(q, k, v)

    t_off = 0 if with_ctx_out else nct
    n_out = ntile - t_off
    sh = lambda last: pl.BlockSpec((1, tm, last), lambda b, t: (b, t + t_off, 0))
    return pl.pallas_call(
        _mla_out_kernel,
        grid=(bsz, n_out),
        in_specs=[tok(hdim * MLA_V), sh(hdim * MLA_V), sh(d),
                  pl.BlockSpec((1, 1, 3, d), lambda b, t: (b, jnp.where(t + t_off < nct, 0, 1), 0, 0)),
                  _const_spec(wuv.shape), _const_spec((hdim * MLA_V, d)),
                  _const_spec((1, d)), _const_spec((1, d))],
        out_specs=tok(d),
        out_shape=jax.ShapeDtypeStruct((bsz, n_out * tm, d), F32),
        compiler_params=_params(("parallel", "parallel")),
        name="mla_out",
    )(o, g, xt, mod, wuv, w_out.astype(BF16), ln_g.reshape(1, d), ln_b.reshape(1, d))


def _rope_tables(n_ctx, n_lat):
    rows = n_lat // GRID_W
    r, col = jnp.meshgrid(jnp.arange(rows, dtype=F32), jnp.arange(GRID_W, dtype=F32), indexing="ij")
    quarter = MLA_ROPE // 4
    inv = ROPE_BASE ** (-jnp.arange(quarter, dtype=F32) / quarter)
    ang_r = r.reshape(-1)[:, None] * inv
    ang_c = col.reshape(-1)[:, None] * inv
    ang = jnp.concatenate([ang_r, ang_r, ang_c, ang_c], axis=-1)
    cos, sin = jnp.cos(ang), jnp.sin(ang)
    one, zero = jnp.ones_like(cos), jnp.zeros_like(cos)
    czero, cone = jnp.zeros((n_ctx, MLA_ROPE), F32), jnp.ones((n_ctx, MLA_ROPE), F32)
    cq = jnp.concatenate([jnp.concatenate([czero, cone], 1), jnp.concatenate([cos, one], 1)], 0)
    sq = jnp.concatenate([jnp.concatenate([czero, czero], 1), jnp.concatenate([sin, zero], 1)], 0)
    ck = jnp.concatenate([jnp.concatenate([czero, cone], 1), jnp.concatenate([cos, zero], 1)], 0)
    return cq, sq, ck


def _s5_in_kernel(x_ref, mod_ref, wut_ref, wg_ref, ut_ref, g_ref):
    h = _modulate(x_ref, mod_ref).astype(BF16)
    g_ref[0] = jnp.dot(h, wg_ref[...], preferred_element_type=F32).astype(BF16)
    ut = lax.dot_general(wut_ref[...], h, (((1,), (1,)), ((), ())), preferred_element_type=F32)
    for s in range(ut_ref.shape[0]):
        ut_ref[s, 0] = ut[:, s * S5_CHUNK:(s + 1) * S5_CHUNK]


def _cexp(scale, ar, ai):
    mag = jnp.exp(scale * ar)
    ang = scale * ai
    return mag * jnp.cos(ang), mag * jnp.sin(ang)


def _s5_scan_kernel(u_ref, lre_r, lim_r, ldt_r, lre_c, lim_c, ldt_c, btr_ref, bti_ref, btilr_ref, btili_ref,
                    crepr_ref, crepi_ref, ctr_ref, cti_ref, dterm_ref, y_ref,
                    lhs_sc, wst_sc, wout_sc, wtoep_sc, v_sc, s_sc, hf_sc, hb_sc, *, nb, order_bwd):
    t_len = S5_CHUNK
    half = S5_STATE
    hi = lax.Precision.HIGHEST
    n_chunks = len(order_bwd)

    ar = jnp.minimum(lre_r[0], -1e-4) * jnp.exp(ldt_r[0])
    ai = lim_r[0] * jnp.exp(ldt_r[0])
    lr, li = jnp.minimum(lre_r[0], -1e-4), lim_r[0]
    lbr, lbi = _cexp(1.0, ar, ai)
    den = lr * lr + li * li
    fr = ((lbr - 1.0) * lr + lbi * li) / den
    fi = (lbi * lr - (lbr - 1.0) * li) / den
    tr, ti = _cexp(float(t_len), ar, ai)

    row = lax.broadcasted_iota(jnp.int32, (t_len, LANE), 0)
    lane = lax.broadcasted_iota(jnp.int32, (t_len, LANE), 1)
    e_s = jnp.where(lane < half, t_len - 1 - row, row).astype(F32)
    es_r, es_i = _cexp(e_s, ar, ai)
    bbr = btr_ref[0] * fr - bti_ref[0] * fi
    bbi = btr_ref[0] * fi + bti_ref[0] * fr
    for i in range(S5_GROUP):
        br, bi = bbr[i:i + 1], bbi[i:i + 1]
        wst_sc[i * t_len:(i + 1) * t_len, 0:LANE] = (es_r * br - es_i * bi).astype(BF16)
        wst_sc[i * t_len:(i + 1) * t_len, LANE:2 * LANE] = (es_r * bi + es_i * br).astype(BF16)

    dt_c = jnp.exp(ldt_c[0])
    ar_c = jnp.minimum(lre_c[0], -1e-4) * dt_c
    ai_c = lim_c[0] * dt_c
    is_f = row < half
    e_k = jnp.where(is_f, lane, t_len - lane).astype(F32)
    pk_r, pk_i = _cexp(e_k, ar_c, ai_c)
    lbr_c, lbi_c = _cexp(1.0, ar_c, ai_c)
    po_r = jnp.where(is_f, pk_r * lbr_c - pk_i * lbi_c, pk_r)
    po_i = jnp.where(is_f, pk_r * lbi_c + pk_i * lbr_c, pk_i)
    ctr, cti = ctr_ref[0], cti_ref[0]
    for o in range(S5_GROUP):
        cr, ci = ctr[:, o:o + 1], cti[:, o:o + 1]
        wout_sc[0:LANE, o * t_len:(o + 1) * t_len] = (cr * po_r - ci * po_i).astype(BF16)
        wout_sc[LANE:2 * LANE, o * t_len:(o + 1) * t_len] = (-(cr * po_i + ci * po_r)).astype(BF16)

    mbr = btilr_ref[0] * fr - btili_ref[0] * fi
    mbi = btilr_ref[0] * fi + btili_ref[0] * fr
    m_r = crepr_ref[0] * mbr - crepi_ref[0] * mbi
    m_i = crepr_ref[0] * mbi + crepi_ref[0] * mbr
    zero = jnp.zeros_like(pk_r)
    pcat_r = jnp.concatenate([jnp.where(is_f, zero, pk_r), jnp.where(is_f, pk_r, zero)], axis=1)
    pcat_i = jnp.concatenate([jnp.where(is_f, zero, pk_i), jnp.where(is_f, pk_i, zero)], axis=1)
    v = (jnp.dot(m_r, pcat_r, preferred_element_type=F32, precision=hi)
         - jnp.dot(m_i, pcat_i, preferred_element_type=F32, precision=hi))
    lane_m = lax.broadcasted_iota(jnp.int32, m_r.shape, 1)
    kb0 = jnp.sum(jnp.where(lane_m >= half, m_r, 0.0), axis=1, keepdims=True)
    lane_v = lax.broadcasted_iota(jnp.int32, v.shape, 1)
    v_sc[...] = v + jnp.where(lane_v == t_len, kb0 + dterm_ref[0], 0.0)

    def toep_o(o, carry):
        vblk = v_sc[pl.ds(pl.multiple_of(o * S5_GROUP, S5_GROUP), S5_GROUP), :]
        col = pl.ds(pl.multiple_of(o * t_len, t_len), t_len)
        for i in range(S5_GROUP):
            row_b = jnp.broadcast_to(vblk[i:i + 1], (t_len, 2 * t_len))
            rolled = pltpu.roll(row_b, 0, 1, stride=1, stride_axis=0)
            wtoep_sc[i * t_len:(i + 1) * t_len, col] = rolled[:, t_len:].astype(BF16)
        return carry
    lax.fori_loop(0, S5_GROUP, toep_o, 0)

    u_t = jnp.swapaxes(u_ref[...], 0, 1)
    for i in range(S5_GROUP):
        lhs_sc[:, i * t_len:(i + 1) * t_len] = u_t[i].astype(BF16)
    lhs = lhs_sc[...]
    s_sc[...] = jnp.dot(lhs, wst_sc[...], preferred_element_type=F32)

    lane_h = lax.broadcasted_iota(jnp.int32, (nb, LANE), 1)
    f_lane = lane_h < half
    h_r = jnp.zeros((nb, LANE), F32)
    h_i = jnp.zeros((nb, LANE), F32)
    for j in range(n_chunks):
        cf, cb = j, order_bwd[j]
        hf_sc[cf * nb:(cf + 1) * nb, 0:LANE] = h_r
        hf_sc[cf * nb:(cf + 1) * nb, LANE:2 * LANE] = h_i
        hb_sc[cb * nb:(cb + 1) * nb, 0:LANE] = h_r
        hb_sc[cb * nb:(cb + 1) * nb, LANE:2 * LANE] = h_i
        s_r = jnp.where(f_lane, s_sc[cf * nb:(cf + 1) * nb, 0:LANE], s_sc[cb * nb:(cb + 1) * nb, 0:LANE])
        s_i = jnp.where(f_lane, s_sc[cf * nb:(cf + 1) * nb, LANE:2 * LANE],
                        s_sc[cb * nb:(cb + 1) * nb, LANE:2 * LANE])
        h_r, h_i = tr * h_r - ti * h_i + s_r, tr * h_i + ti * h_r + s_i
    lane_2 = lax.broadcasted_iota(jnp.int32, hf_sc.shape, 1)
    h_in = jnp.where((lane_2 % LANE) < half, hf_sc[...], hb_sc[...])
    h_hi = h_in.astype(BF16)
    h_lo = (h_in - h_hi.astype(F32)).astype(BF16)
    wout = wout_sc[...]
    y = (jnp.dot(lhs, wtoep_sc[...], preferred_element_type=F32)
         + jnp.dot(h_hi, wout, preferred_element_type=F32)
         + jnp.dot(h_lo, wout, preferred_element_type=F32))
    y_t = jnp.stack([y[:, o * t_len:(o + 1) * t_len] for o in range(S5_GROUP)], axis=0)
    y_ref[...] = jnp.swapaxes(y_t, 0, 1)


def _s5_out_kernel(yt_ref, g_ref, x_ref, mod_ref, wglu_ref, bglu_ref, wout_ref, lng_ref, lnb_ref, out_ref):
    parts = [jnp.transpose(_gelu_tanh(yt_ref[s, 0])).astype(BF16) for s in range(yt_ref.shape[0])]
    a = jnp.concatenate(parts, axis=0)
    z = jnp.dot(a, wglu_ref[...], preferred_element_type=F32) + bglu_ref[...]
    t = z[:, :E_BRANCH] * _sigmoid(z[:, E_BRANCH:]) * _silu(g_ref[0].astype(F32))
    y = jnp.dot(t.astype(BF16), wout_ref[...], preferred_element_type=F32)
    out_ref[0] = _residual_norm(x_ref, mod_ref, y, lng_ref, lnb_ref)


def _s5_layer(xt, mod, n_ctx, w_in, lam_re, lam_im, log_dt, b_re, b_im, c_re, c_im, d_skip, w_glu, b_glu, w_out,
              ln_g, ln_b):
    bsz, nt, d = xt.shape
    tm = TOK_TILE
    e, ng, gs, ns, tl = E_BRANCH, S5_GROUPS, S5_GROUP, S5_STATE, S5_CHUNK
    nct, ntile = n_ctx // tm, nt // tm
    nsub = tm // tl
    nch, nch_ctx = nt // tl, n_ctx // tl
    cb = nch * bsz
    assert n_ctx % tm == 0 and nt % tm == 0 and cb % 16 == 0 and bsz % 8 == 0

    tok = lambda last: pl.BlockSpec((1, tm, last), lambda b, t: (b, t, 0))
    ut, g = pl.pallas_call(
        _s5_in_kernel,
        grid=(bsz, ntile),
        in_specs=[tok(d), _mod_spec(nct), _const_spec((e, d)), _const_spec((d, e))],
        out_specs=[pl.BlockSpec((nsub, 1, e, tl), lambda b, t: (t, b, 0, 0)), tok(e)],
        out_shape=[jax.ShapeDtypeStruct((nch, bsz, e, tl), F32), jax.ShapeDtypeStruct((bsz, nt, e), BF16)],
        compiler_params=_params(("parallel", "parallel")),
        name="s5_in",
    )(xt, mod, jnp.transpose(w_in[:, :e]).astype(BF16), w_in[:, e:].astype(BF16))

    both = lambda a: jnp.concatenate([a[0], a[1]], axis=-1)
    lre, lim = both(lam_re), both(lam_im)
    ldt = jnp.repeat(jnp.transpose(log_dt), ns, axis=1)
    bt_r, bt_i = (both(jnp.swapaxes(a, 2, 3)) for a in (b_re, b_im))
    cr_r, cr_i = both(c_re), both(c_im)
    btil_r, btil_i = (jnp.tile(a, (1, gs, 1)) for a in (bt_r, bt_i))
    crep_r, crep_i = (jnp.repeat(a, gs, axis=1) for a in (cr_r, cr_i))
    ct_r, ct_i = (jnp.swapaxes(a, 1, 2) for a in (cr_r, cr_i))
    dterm = (d_skip.reshape(ng, gs, 1) * jnp.eye(gs, dtype=F32)).reshape(ng, gs * gs, 1)

    order_bwd = tuple(range(nch_ctx - 1, -1, -1)) + tuple(range(nch - 1, nch_ctx - 1, -1))
    row_spec = pl.BlockSpec((1, 1, 2 * ns), lambda gi: (gi, 0, 0))
    col_spec = pl.BlockSpec((1, 2 * ns, 1), lambda gi: (gi, 0, 0))
    g3 = lambda r, c: pl.BlockSpec((1, r, c), lambda gi: (gi, 0, 0))
    data_spec = pl.BlockSpec((cb, gs, tl), lambda gi: (0, gi, 0))
    yt = pl.pallas_call(
        functools.partial(_s5_scan_kernel, nb=bsz, order_bwd=order_bwd),
        grid=(ng,),
        in_specs=[data_spec, row_spec, row_spec, row_spec, col_spec, col_spec, col_spec,
                  g3(gs, 2 * ns), g3(gs, 2 * ns), g3(gs * gs, 2 * ns), g3(gs * gs, 2 * ns),
                  g3(gs * gs, 2 * ns), g3(gs * gs, 2 * ns), g3(2 * ns, gs), g3(2 * ns, gs), g3(gs * gs, 1)],
        out_specs=data_spec,
        out_shape=jax.ShapeDtypeStruct((cb, e, tl), F32),
        scratch_shapes=[pltpu.VMEM((cb, gs * tl), BF16), pltpu.VMEM((gs * tl, 4 * ns), BF16),
                        pltpu.VMEM((4 * ns, gs * tl), BF16), pltpu.VMEM((gs * tl, gs * tl), BF16),
                        pltpu.VMEM((gs * gs, 2 * tl), F32), pltpu.VMEM((cb, 4 * ns), F32),
                        pltpu.VMEM((cb, 4 * ns), F32), pltpu.VMEM((cb, 4 * ns), F32)],
        compiler_params=_params(("parallel",)),
        name="s5_scan",
    )(ut.reshape(cb, e, tl), lre[:, None, :], lim[:, None, :], ldt[:, None, :],
      lre[:, :, None], lim[:, :, None], ldt[:, :, None],
      bt_r, bt_i, btil_r, btil_i, crep_r, crep_i, ct_r, ct_i, dterm)

    return pl.pallas_call(
        _s5_out_kernel,
        grid=(bsz, ntile),
        in_specs=[pl.BlockSpec((nsub, 1, e, tl), lambda b, t: (t, b, 0, 0)), tok(e), tok(d), _mod_spec(nct),
                  _const_spec((e, 2 * e)), _const_spec((1, 2 * e)), _const_spec((e, d)),
                  _const_spec((1, d)), _const_spec((1, d))],
        out_specs=tok(d),
        out_shape=jax.ShapeDtypeStruct((bsz, nt, d), F32),
        compiler_params=_params(("parallel", "parallel")),
        name="s5_out",
    )(yt.reshape(nch, bsz, e, tl), g, xt, mod, w_glu.astype(BF16), b_glu.reshape(1, -1), w_out.astype(BF16),
      ln_g.reshape(1, d), ln_b.reshape(1, d))


def _conv_in_kernel(x_ref, mod_ref, w_ref, v_ref, g_ref):
    h = _modulate(x_ref, mod_ref).astype(BF16)
    z = jnp.dot(h, w_ref[...], preferred_element_type=F32)
    v_ref[0] = (z[:, :E_BRANCH] * _sigmoid(z[:, E_BRANCH:2 * E_BRANCH])).astype(BF16)
    g_ref[0] = z[:, 2 * E_BRANCH:].astype(BF16)


def _conv_out_kernel(v_ref, vp_ref, vn_ref, g_ref, x_ref, mod_ref, dw_ref, dwb_ref, cg_ref, cb_ref, wout_ref,
                     lng_ref, lnb_ref, out_ref, vext_sc, y_sc, *, nct, ntile, row_block):
    tm = v_ref.shape[1]
    t = pl.program_id(1)
    has_prev = jnp.logical_and(t != 0, t != nct)
    has_next = jnp.logical_and(t != nct - 1, t != ntile - 1)
    vext_sc[0:CONV_HALO] = jnp.where(has_prev, vp_ref[0].astype(F32), 0.0)
    vext_sc[CONV_HALO:CONV_HALO + tm] = v_ref[0].astype(F32)
    vext_sc[CONV_HALO + tm:2 * CONV_HALO + tm] = jnp.where(has_next, vn_ref[0].astype(F32), 0.0)

    shift0 = CONV_HALO - CONV_PAD

    def lane_block(cbk, carry):
        lo = pl.multiple_of(cbk * LANE, LANE)
        w = dw_ref[:, pl.ds(lo, LANE)]
        bias = dwb_ref[:, pl.ds(lo, LANE)]
        for rb in range(tm // row_block):
            acc = jnp.zeros((row_block, LANE), F32)
            for rho in range(SUBLANE):
                z = None
                for k in range(CONV_WIDTH):
                    if (k + shift0) % SUBLANE != rho:
                        continue
                    start = rb * row_block + (k + shift0) - rho
                    term = vext_sc[pl.ds(start, row_block + SUBLANE), pl.ds(lo, LANE)] * w[k:k + 1]
                    z = term if z is None else z + term
                if z is not None:
                    acc = acc + z[rho:rho + row_block]
            y_sc[rb * row_block:(rb + 1) * row_block, pl.ds(lo, LANE)] = acc + bias
        return carry
    lax.fori_loop(0, E_BRANCH // LANE, lane_block, 0)

    c = _silu(_layernorm(y_sc[...], cg_ref[...], cb_ref[...]))
    y = jnp.dot((c * _silu(g_ref[0].astype(F32))).astype(BF16), wout_ref[...], preferred_element_type=F32)
    out_ref[0] = _residual_norm(x_ref, mod_ref, y, lng_ref, lnb_ref)


def _conv_layer(xt, mod, n_ctx, w_in, dw, dw_b, cln_g, cln_b, w_out, ln_g, ln_b):
    bsz, nt, d = xt.shape
    tm, e = TOK_TILE, E_BRANCH
    nct, ntile = n_ctx // tm, nt // tm
    assert n_ctx % tm == 0 and nt % tm == 0
    tok = lambda last: pl.BlockSpec((1, tm, last), lambda b, t: (b, t, 0))
    v, g = pl.pallas_call(
        _conv_in_kernel,
        grid=(bsz, ntile),
        in_specs=[tok(d), _mod_spec(nct), _const_spec((d, 3 * e))],
        out_specs=[tok(e), tok(e)],
        out_shape=[jax.ShapeDtypeStruct((bsz, nt, e), BF16), jax.ShapeDtypeStruct((bsz, nt, e), BF16)],
        compiler_params=_params(("parallel", "parallel")),
        name="conv_in",
    )(xt, mod, w_in.astype(BF16))

    hb = tm // CONV_HALO
    nhb = nt // CONV_HALO
    dw_pad = jnp.concatenate([dw, jnp.zeros((32 - CONV_WIDTH, e), F32)], axis=0)
    return pl.pallas_call(
        functools.partial(_conv_out_kernel, nct=nct, ntile=ntile, row_block=64),
        grid=(bsz, ntile),
        in_specs=[tok(e),
                  pl.BlockSpec((1, CONV_HALO, e), lambda b, t: (b, jnp.maximum(t * hb - 1, 0), 0)),
                  pl.BlockSpec((1, CONV_HALO, e), lambda b, t: (b, jnp.minimum((t + 1) * hb, nhb - 1), 0)),
                  tok(e), tok(d), _mod_spec(nct),
                  _const_spec((32, e)), _const_spec((1, e)), _const_spec((1, e)), _const_spec((1, e)),
                  _const_spec((e, d)), _const_spec((1, d)), _const_spec((1, d))],
        out_specs=tok(d),
        out_shape=jax.ShapeDtypeStruct((bsz, nt, d), F32),
        scratch_shapes=[pltpu.VMEM((tm + 2 * CONV_HALO, e), F32), pltpu.VMEM((tm, e), F32)],
        compiler_params=_params(("parallel", "parallel")),
        name="conv_out",
    )(v, v, v, g, xt, mod, dw_pad, dw_b.reshape(1, e), cln_g.reshape(1, e), cln_b.reshape(1, e),
      w_out.astype(BF16), ln_g.reshape(1, d), ln_b.reshape(1, d))


def kernel(x, c, ctx, c_ctx, w_mod, b_mod, ln_g, ln_b, mla_w_in, mla_q_norm, mla_kv_norm, mla_w_uq, mla_w_uk, mla_w_uv, mla_w_out, s5_w_in, s5_lam_re, s5_lam_im, s5_log_dt, s5_b_re, s5_b_im, s5_c_re, s5_c_im, s5_d, s5_w_glu, s5_b_glu, s5_w_out, cv_w_in, cv_dw, cv_dw_b, cv_ln_g, cv_ln_b, cv_w_out):
    n_ctx, n_lat = ctx.shape[1], x.shape[1]
    mod = _modulation(c, c_ctx, w_mod, b_mod)
    tables = _rope_tables(n_ctx, n_lat)
    xt = jnp.concatenate([ctx, x], axis=1)
    for i in range(DEPTH):
        kind, j = i % N_MIXERS, i // N_MIXERS
        if kind == 0:
            xt = _mla_layer(xt, mod[i], tables, n_ctx, mla_w_in[j], mla_q_norm[j], mla_kv_norm[j], mla_w_uq[j],
                            mla_w_uk[j], mla_w_uv[j], mla_w_out[j], ln_g[i], ln_b[i], i < DEPTH - 1)
        elif kind == 1:
            xt = _s5_layer(xt, mod[i], n_ctx, s5_w_in[j], s5_lam_re[j], s5_lam_im[j], s5_log_dt[j], s5_b_re[j],
                           s5_b_im[j], s5_c_re[j], s5_c_im[j], s5_d[j], s5_w_glu[j], s5_b_glu[j], s5_w_out[j],
                           ln_g[i], ln_b[i])
        else:
            xt = _conv_layer(xt, mod[i], n_ctx, cv_w_in[j], cv_dw[j], cv_dw_b[j], cv_ln_g[j], cv_ln_b[j],
                             cv_w_out[j], ln_g[i], ln_b[i])
    return xt
```

```python
import functools
import math

import jax
import jax.numpy as jnp
from jax import lax
from jax.experimental import pallas as pl
from jax.experimental.pallas import tpu as pltpu

F32 = jnp.float32
BF16 = jnp.bfloat16

D_MODEL = 1024
DEPTH = 4
N_MIXERS = 3
GRID_W = 64
E_BRANCH = 2 * D_MODEL
MLA_HEADS = 16
MLA_NOPE = 128
MLA_ROPE = 64
MLA_V = 128
MLA_Q_RANK = 256
MLA_KV_RANK = 128
MLA_KEY = 256
MLA_VT_ROWS = 144
MLA_QCOLS = 384
ROPE_BASE = 10000.0
S5_GROUP = 16
S5_GROUPS = E_BRANCH // S5_GROUP
S5_STATE = 64
S5_CHUNK = 128
CONV_WIDTH = 31
CONV_PAD = CONV_WIDTH // 2
CONV_HALO = 16
DEEPNORM_ALPHA = (2.0 * DEPTH) ** 0.25
NORM_EPS = 1e-6
LANE = 128
SUBLANE = 8
VMEM_LIMIT = 56 * 1024 * 1024

TOK_TILE = 256
ATT_TQ = 128
ATT_TK = 512
MOD_ROWS = 24


def _sigmoid(x):
    return 1.0 / (1.0 + jnp.exp(-x))


def _silu(x):
    return x * _sigmoid(x)


def _gelu_tanh(x):
    return 0.5 * x * (1.0 + jnp.tanh(math.sqrt(2.0 / math.pi) * (x + 0.044715 * (x * x * x))))


def _layernorm(r, g, b):
    mu = jnp.mean(r, axis=-1, keepdims=True)
    d = r - mu
    var = jnp.mean(d * d, axis=-1, keepdims=True)
    return d * lax.rsqrt(var + NORM_EPS) * g + b


def _rmsnorm(x, g):
    return x * lax.rsqrt(jnp.mean(x * x, axis=-1, keepdims=True) + NORM_EPS) * g


def _params(sem):
    return pltpu.CompilerParams(dimension_semantics=sem, vmem_limit_bytes=VMEM_LIMIT)


def _const_spec(shape):
    nd = len(shape)
    return pl.BlockSpec(shape, lambda *_: (0,) * nd, pipeline_mode=pl.Buffered(1))


def _mod_kernel(c_ref, w_ref, b_ref, o_ref):
    c = c_ref[...]
    o_ref[0] = jnp.dot(_silu(c), w_ref[0], preferred_element_type=F32,
                       precision=lax.Precision.HIGHEST) + b_ref[0]


def _modulation(c, c_ctx, w_mod, b_mod):
    bsz, d = c.shape
    assert bsz + 1 <= MOD_ROWS
    rows = jnp.concatenate([c, c_ctx[None], jnp.zeros((MOD_ROWS - bsz - 1, d), F32)], axis=0)
    out = pl.pallas_call(
        _mod_kernel,
        grid=(DEPTH, 3),
        in_specs=[pl.BlockSpec((MOD_ROWS, d), lambda i, j: (0, 0)),
                  pl.BlockSpec((1, d, d), lambda i, j: (i, 0, j)),
                  pl.BlockSpec((1, 1, d), lambda i, j: (i, 0, j))],
        out_specs=pl.BlockSpec((1, MOD_ROWS, d), lambda i, j: (i, 0, j)),
        out_shape=jax.ShapeDtypeStruct((DEPTH, MOD_ROWS, 3 * d), F32),
        compiler_params=_params(("parallel", "parallel")),
        name="modulation",
    )(rows, w_mod, b_mod.reshape(DEPTH, 1, 3 * d))
    out = out.reshape(DEPTH, MOD_ROWS, 3, d)
    lat = out[:, :bsz]
    ctx = jnp.broadcast_to(out[:, bsz][:, None], lat.shape)
    return jnp.stack([ctx, lat], axis=2)


def _mod_spec(nct):
    return pl.BlockSpec((1, 1, 3, D_MODEL), lambda b, t: (b, jnp.where(t < nct, 0, 1), 0, 0))


def _modulate(x_ref, mod_ref):
    return x_ref[0] * (1.0 + mod_ref[0, 0, 1:2, :]) + mod_ref[0, 0, 0:1, :]


def _residual_norm(x_ref, mod_ref, y, lng_ref, lnb_ref):
    r = DEEPNORM_ALPHA * x_ref[0] + mod_ref[0, 0, 2:3, :] * y
    return _layernorm(r, lng_ref[...], lnb_ref[...])


def _rot_cols(w):
    x1, x2, x3, x4 = jnp.split(w, 4, axis=-1)
    return jnp.concatenate([-x2, x1, -x4, x3], axis=-1)


def _fold_kernel(uq_ref, uk_ref, o_ref):
    o_ref[0] = jnp.dot(uq_ref[0], uk_ref[0], preferred_element_type=F32, precision=lax.Precision.HIGHEST)


def _mla_in_kernel(x_ref, mod_ref, cq_ref, sq_ref, ck_ref, w1_ref, w2_ref, qn_ref, kvn_ref,
                   q_ref, k_ref, vt_ref, g_ref, *, tq, qscale):
    tm = x_ref.shape[1]
    h = _modulate(x_ref, mod_ref).astype(BF16)
    z = jnp.dot(h, w1_ref[...], preferred_element_type=F32)
    qn = _rmsnorm(z[:, 0:MLA_Q_RANK], qn_ref[...]).astype(BF16)
    ckv = _rmsnorm(z[:, 256:384], kvn_ref[...])
    sq = sq_ref[...]
    kk = z[:, 384:512] * ck_ref[...] + z[:, 512:640] * sq
    k_ref[0, :, 0:LANE] = ckv.astype(BF16)
    k_ref[0, :, LANE:2 * LANE] = kk.astype(BF16)
    vt_ref[0, 0:MLA_KV_RANK, :] = jnp.transpose(ckv).astype(BF16)
    row = lax.broadcasted_iota(jnp.int32, (MLA_VT_ROWS - MLA_KV_RANK, tm), 0)
    vt_ref[0, MLA_KV_RANK:MLA_VT_ROWS, :] = jnp.where(row == 0, 1.0, 0.0).astype(BF16)
    g_ref[0] = z[:, 640:].astype(BF16)
    qq = jnp.dot(qn, w2_ref[...], preferred_element_type=F32)
    cq = cq_ref[...]
    for hh in range(MLA_HEADS):
        base = hh * MLA_QCOLS
        qa = qq[:, base:base + LANE] * qscale
        qr = (qq[:, base + LANE:base + 2 * LANE] * cq + qq[:, base + 2 * LANE:base + 3 * LANE] * sq) * qscale
        for s in range(tm // tq):
            q_ref[0, s, hh * tq:(hh + 1) * tq, 0:LANE] = qa[s * tq:(s + 1) * tq].astype(BF16)
            q_ref[0, s, hh * tq:(hh + 1) * tq, LANE:2 * LANE] = qr[s * tq:(s + 1) * tq].astype(BF16)


def _attn_kernel(q_ref, k_ref, vt_ref, o_ref, m_sc, acc_sc, sa_sc, sb_sc, *, tq, tk, q_off, nct_q, n_ctx, n_lat_chunks):
    m_sc[...] = jnp.full(m_sc.shape, -1e30, F32)
    acc_sc[...] = jnp.zeros(acc_sc.shape, F32)
    q = q_ref[0, 0]

    def scores(off, size):
        kc = k_ref[0, pl.ds(off, size), :]
        return lax.dot_general(kc, q, (((1,), (1,)), ((), ())), preferred_element_type=F32)

    def update(s, off, size):
        m_prev = m_sc[...]
        m_new = jnp.maximum(m_prev, jnp.max(s, axis=0, keepdims=True))
        alpha = jnp.exp2(m_prev - m_new)
        p = jnp.exp2(s - m_new).astype(BF16)
        pv = jnp.dot(vt_ref[0, :, pl.ds(off, size)], p, preferred_element_type=F32)
        acc_sc[...] = alpha * acc_sc[...] + pv
        m_sc[...] = m_new

    def lat_off(c):
        return pl.multiple_of(n_ctx + c * tk, LANE)

    def latent_chunks():
        n = n_lat_chunks
        if n == 1:
            update(scores(n_ctx, tk), n_ctx, tk)
            return
        sa_sc[...] = scores(n_ctx, tk)

        def pair(i, carry):
            sb_sc[...] = scores(lat_off(2 * i + 1), tk)
            update(sa_sc[...], lat_off(2 * i), tk)
            sa_sc[...] = scores(lat_off(2 * i + 2), tk)
            update(sb_sc[...], lat_off(2 * i + 1), tk)
            return carry

        npairs = n // 2
        if npairs > 1:
            lax.fori_loop(0, npairs - 1, pair, 0)
        last = 2 * (npairs - 1)
        sb_sc[...] = scores(n_ctx + (last + 1) * tk, tk)
        update(sa_sc[...], n_ctx + last * tk, tk)
        if n % 2:
            sa_sc[...] = scores(n_ctx + (last + 2) * tk, tk)
        update(sb_sc[...], n_ctx + (last + 1) * tk, tk)
        if n % 2:
            update(sa_sc[...], n_ctx + (last + 2) * tk, tk)

    update(scores(0, n_ctx), 0, n_ctx)
    if q_off >= nct_q:
        latent_chunks()
    else:
        pl.when(pl.program_id(1) + q_off >= nct_q)(latent_chunks)

    ot = acc_sc[0:MLA_KV_RANK, :] * (1.0 / acc_sc[MLA_KV_RANK:MLA_KV_RANK + 1, :])
    for hh in range(MLA_HEADS):
        o_ref[0, :, hh * LANE:(hh + 1) * LANE] = jnp.transpose(ot[:, hh * tq:(hh + 1) * tq]).astype(BF16)


def _mla_out_kernel(o_ref, g_ref, x_ref, mod_ref, wuv_ref, wout_ref, lng_ref, lnb_ref, out_ref):
    parts = []
    for j in range(MLA_HEADS // 2):
        cols = slice(j * 2 * LANE, (j + 1) * 2 * LANE)
        v = jnp.dot(o_ref[0, :, cols], wuv_ref[j], preferred_element_type=F32)
        parts.append((v * _silu(g_ref[0, :, cols].astype(F32))).astype(BF16))
    vg = jnp.concatenate(parts, axis=1)
    y = jnp.dot(vg, wout_ref[...], preferred_element_type=F32)
    out_ref[0] = _residual_norm(x_ref, mod_ref, y, lng_ref, lnb_ref)


def _mla_layer(xt, mod, tables, n_ctx, w_in, q_norm, kv_norm, w_uq, w_uk, w_uv, w_out, ln_g, ln_b, with_ctx_out):
    bsz, nt, d = xt.shape
    tm, tq = TOK_TILE, ATT_TQ
    tk = ATT_TK
    assert n_ctx % tm == 0 and nt % tm == 0 and n_ctx <= ATT_TK and (nt - n_ctx) % tk == 0
    nct = n_ctx // tm
    ntile = nt // tm
    hdim = MLA_HEADS
    qscale = (MLA_NOPE + MLA_ROPE) ** -0.5 * math.log2(math.e)

    kr = w_in[:, 384:448]
    w1 = jnp.concatenate([w_in[:, 0:384], kr, kr, _rot_cols(kr), jnp.zeros_like(kr), w_in[:, 448:]],
                         axis=1).astype(BF16)
    uq = w_uq.reshape(MLA_Q_RANK, hdim, MLA_NOPE + MLA_ROPE)
    fold = pl.pallas_call(
        _fold_kernel,
        grid=(hdim,),
        in_specs=[pl.BlockSpec((1, MLA_Q_RANK, MLA_NOPE), lambda h: (h, 0, 0)),
                  pl.BlockSpec((1, MLA_NOPE, MLA_KV_RANK), lambda h: (h, 0, 0))],
        out_specs=pl.BlockSpec((1, MLA_Q_RANK, MLA_KV_RANK), lambda h: (h, 0, 0)),
        out_shape=jax.ShapeDtypeStruct((hdim, MLA_Q_RANK, MLA_KV_RANK), F32),
        compiler_params=_params(("parallel",)),
        name="mla_fold",
    )(jnp.transpose(uq[:, :, :MLA_NOPE], (1, 0, 2)), jnp.transpose(w_uk, (1, 2, 0)))
    rp = uq[:, :, MLA_NOPE:]
    w2 = jnp.concatenate([jnp.transpose(fold, (1, 0, 2)), rp, rp, _rot_cols(rp), jnp.zeros_like(rp)],
                         axis=2).reshape(MLA_Q_RANK, hdim * MLA_QCOLS).astype(BF16)
    wuv = jnp.transpose(w_uv, (1, 0, 2))
    zero = jnp.zeros_like(wuv[0::2])
    wuv2 = jnp.concatenate([jnp.concatenate([wuv[0::2], zero], axis=2),
                            jnp.concatenate([zero, wuv[1::2]], axis=2)], axis=1).astype(BF16)
    cq, sq, ck = tables

    tok = lambda last: pl.BlockSpec((1, tm, last), lambda b, t: (b, t, 0))
    tab = pl.BlockSpec((tm, LANE), lambda b, t: (t, 0))
    q, k, vt, g = pl.pallas_call(
        functools.partial(_mla_in_kernel, tq=tq, qscale=qscale),
        grid=(bsz, ntile),
        in_specs=[tok(d), _mod_spec(nct), tab, tab, tab,
                  _const_spec(w1.shape), _const_spec(w2.shape),
                  _const_spec((1, MLA_Q_RANK)), _const_spec((1, MLA_KV_RANK))],
        out_specs=[pl.BlockSpec((1, tm // tq, hdim * tq, MLA_KEY), lambda b, t: (b, t, 0, 0)),
                   tok(MLA_KEY), pl.BlockSpec((1, MLA_VT_ROWS, tm), lambda b, t: (b, 0, t)), tok(hdim * MLA_V)],
        out_shape=[jax.ShapeDtypeStruct((bsz, nt // tq, hdim * tq, MLA_KEY), BF16),
                   jax.ShapeDtypeStruct((bsz, nt, MLA_KEY), BF16),
                   jax.ShapeDtypeStruct((bsz, MLA_VT_ROWS, nt), BF16),
                   jax.ShapeDtypeStruct((bsz, nt, hdim * MLA_V), BF16)],
        compiler_params=_params(("parallel", "parallel")),
        name="mla_in",
    )(xt, mod, cq, sq, ck, w1, w2, q_norm.reshape(1, -1), kv_norm.reshape(1, -1))

    q_off = 0 if with_ctx_out else n_ctx // tq
    nq = nt // tq - q_off
    rows = hdim * tq
    o = pl.pallas_call(
        functools.partial(_attn_kernel, tq=tq, tk=tk, q_off=q_off, nct_q=n_ctx // tq, n_ctx=n_ctx,
                          n_lat_chunks=(nt - n_ctx) // tk),
        grid=(bsz, nq),
        in_specs=[pl.BlockSpec((1, 1, rows, MLA_KEY), lambda b, t: (b, t + q_off, 0, 0)),
                  pl.BlockSpec((1, nt, MLA_KEY), lambda b, t: (b, 0, 0)),
                  pl.BlockSpec((1, MLA_VT_ROWS, nt), lambda b, t: (b, 0, 0))],
        out_specs=pl.BlockSpec((1, tq, hdim * MLA_V), lambda b, t: (b, t, 0)),
        out_shape=jax.ShapeDtypeStruct((bsz, nq * tq, hdim * MLA_V), BF16),
        scratch_shapes=[pltpu.VMEM((1, rows), F32), pltpu.VMEM((MLA_VT_ROWS, rows), F32),
                        pltpu.VMEM((tk, rows), F32), pltpu.VMEM((tk, rows), F32)],
        compiler_params=_params(("parallel", "parallel")),
        name="mla_attention",
    )(q, k, vt)

    t_off = 0 if with_ctx_out else nct
    n_out = ntile - t_off
    sh = lambda last: pl.BlockSpec((1, tm, last), lambda b, t: (b, t + t_off, 0))
    return pl.pallas_call(
        _mla_out_kernel,
        grid=(bsz, n_out),
        in_specs=[tok(hdim * MLA_V), sh(hdim * MLA_V), sh(d),
                  pl.BlockSpec((1, 1, 3, d), lambda b, t: (b, jnp.where(t + t_off < nct, 0, 1), 0, 0)),
                  _const_spec(wuv2.shape), _const_spec((hdim * MLA_V, d)),
                  _const_spec((1, d)), _const_spec((1, d))],
        out_specs=tok(d),
        out_shape=jax.ShapeDtypeStruct((bsz, n_out * tm, d), F32),
        compiler_params=_params(("parallel", "parallel")),
        name="mla_out",
    )(o, g, xt, mod, wuv2, w_out.astype(BF16), ln_g.reshape(1, d), ln_b.reshape(1, d))


def _rope_tables(n_ctx, n_lat):
    rows = n_lat // GRID_W
    r, col = jnp.meshgrid(jnp.arange(rows, dtype=F32), jnp.arange(GRID_W, dtype=F32), indexing="ij")
    quarter = MLA_ROPE // 4
    inv = ROPE_BASE ** (-jnp.arange(quarter, dtype=F32) / quarter)
    ang_r = r.reshape(-1)[:, None] * inv
    ang_c = col.reshape(-1)[:, None] * inv
    ang = jnp.concatenate([ang_r, ang_r, ang_c, ang_c], axis=-1)
    cos, sin = jnp.cos(ang), jnp.sin(ang)
    one, zero = jnp.ones_like(cos), jnp.zeros_like(cos)
    czero, cone = jnp.zeros((n_ctx, MLA_ROPE), F32), jnp.ones((n_ctx, MLA_ROPE), F32)
    cq = jnp.concatenate([jnp.concatenate([czero, cone], 1), jnp.concatenate([cos, one], 1)], 0)
    sq = jnp.concatenate([jnp.concatenate([czero, czero], 1), jnp.concatenate([sin, zero], 1)], 0)
    ck = jnp.concatenate([jnp.concatenate([czero, cone], 1), jnp.concatenate([cos, zero], 1)], 0)
    return cq, sq, ck


def _s5_in_kernel(x_ref, mod_ref, wut_ref, wg_ref, ut_ref, g_ref):
    h = _modulate(x_ref, mod_ref).astype(BF16)
    g_ref[0] = jnp.dot(h, wg_ref[...], preferred_element_type=F32).astype(BF16)
    ut = lax.dot_general(wut_ref[...], h, (((1,), (1,)), ((), ())), preferred_element_type=F32)
    for s in range(ut_ref.shape[0]):
        ut_ref[s, 0] = ut[:, s * S5_CHUNK:(s + 1) * S5_CHUNK]


def _cexp(scale, ar, ai):
    mag = jnp.exp(scale * ar)
    ang = scale * ai
    return mag * jnp.cos(ang), mag * jnp.sin(ang)


def _s5_scan_kernel(u_ref, lre_r, lim_r, ldt_r, lre_c, lim_c, ldt_c, btr_ref, bti_ref, btilr_ref, btili_ref,
                    crepr_ref, crepi_ref, ctr_ref, cti_ref, dterm_ref, y_ref,
                    lhs_sc, wst_sc, wout_sc, wtoep_sc, v_sc, s_sc, hf_sc, hb_sc, *, nb, order_bwd):
    t_len = S5_CHUNK
    half = S5_STATE
    hi = lax.Precision.HIGHEST
    n_chunks = len(order_bwd)

    ar = jnp.minimum(lre_r[0], -1e-4) * jnp.exp(ldt_r[0])
    ai = lim_r[0] * jnp.exp(ldt_r[0])
    lr, li = jnp.minimum(lre_r[0], -1e-4), lim_r[0]
    lbr, lbi = _cexp(1.0, ar, ai)
    den = lr * lr + li * li
    fr = ((lbr - 1.0) * lr + lbi * li) / den
    fi = (lbi * lr - (lbr - 1.0) * li) / den
    tr, ti = _cexp(float(t_len), ar, ai)

    row = lax.broadcasted_iota(jnp.int32, (t_len, LANE), 0)
    lane = lax.broadcasted_iota(jnp.int32, (t_len, LANE), 1)
    e_s = jnp.where(lane < half, t_len - 1 - row, row).astype(F32)
    es_r, es_i = _cexp(e_s, ar, ai)
    bbr = btr_ref[0] * fr - bti_ref[0] * fi
    bbi = btr_ref[0] * fi + bti_ref[0] * fr
    for i in range(S5_GROUP):
        br, bi = bbr[i:i + 1], bbi[i:i + 1]
        wst_sc[i * t_len:(i + 1) * t_len, 0:LANE] = (es_r * br - es_i * bi).astype(BF16)
        wst_sc[i * t_len:(i + 1) * t_len, LANE:2 * LANE] = (es_r * bi + es_i * br).astype(BF16)

    dt_c = jnp.exp(ldt_c[0])
    ar_c = jnp.minimum(lre_c[0], -1e-4) * dt_c
    ai_c = lim_c[0] * dt_c
    is_f = row < half
    e_k = jnp.where(is_f, lane, t_len - lane).astype(F32)
    pk_r, pk_i = _cexp(e_k, ar_c, ai_c)
    lbr_c, lbi_c = _cexp(1.0, ar_c, ai_c)
    po_r = jnp.where(is_f, pk_r * lbr_c - pk_i * lbi_c, pk_r)
    po_i = jnp.where(is_f, pk_r * lbi_c + pk_i * lbr_c, pk_i)
    ctr, cti = ctr_ref[0], cti_ref[0]
    for o in range(S5_GROUP):
        cr, ci = ctr[:, o:o + 1], cti[:, o:o + 1]
        wout_sc[0:LANE, o * t_len:(o + 1) * t_len] = (cr * po_r - ci * po_i).astype(BF16)
        wout_sc[LANE:2 * LANE, o * t_len:(o + 1) * t_len] = (-(cr * po_i + ci * po_r)).astype(BF16)

    mbr = btilr_ref[0] * fr - btili_ref[0] * fi
    mbi = btilr_ref[0] * fi + btili_ref[0] * fr
    m_r = crepr_ref[0] * mbr - crepi_ref[0] * mbi
    m_i = crepr_ref[0] * mbi + crepi_ref[0] * mbr
    zero = jnp.zeros_like(pk_r)
    pcat_r = jnp.concatenate([jnp.where(is_f, zero, pk_r), jnp.where(is_f, pk_r, zero)], axis=1)
    pcat_i = jnp.concatenate([jnp.where(is_f, zero, pk_i), jnp.where(is_f, pk_i, zero)], axis=1)
    v = (jnp.dot(m_r, pcat_r, preferred_element_type=F32, precision=hi)
         - jnp.dot(m_i, pcat_i, preferred_element_type=F32, precision=hi))
    lane_m = lax.broadcasted_iota(jnp.int32, m_r.shape, 1)
    kb0 = jnp.sum(jnp.where(lane_m >= half, m_r, 0.0), axis=1, keepdims=True)
    lane_v = lax.broadcasted_iota(jnp.int32, v.shape, 1)
    v_sc[...] = v + jnp.where(lane_v == t_len, kb0 + dterm_ref[0], 0.0)

    def toep_o(o, carry):
        vblk = v_sc[pl.ds(pl.multiple_of(o * S5_GROUP, S5_GROUP), S5_GROUP), :]
        col = pl.ds(pl.multiple_of(o * t_len, t_len), t_len)
        for i in range(S5_GROUP):
            row_b = jnp.broadcast_to(vblk[i:i + 1], (t_len, 2 * t_len))
            rolled = pltpu.roll(row_b, 0, 1, stride=1, stride_axis=0)
            wtoep_sc[i * t_len:(i + 1) * t_len, col] = rolled[:, t_len:].astype(BF16)
        return carry
    lax.fori_loop(0, S5_GROUP, toep_o, 0)

    u_t = jnp.swapaxes(u_ref[...], 0, 1)
    for i in range(S5_GROUP):
        lhs_sc[:, i * t_len:(i + 1) * t_len] = u_t[i].astype(BF16)
    lhs = lhs_sc[...]
    s_sc[...] = jnp.dot(lhs, wst_sc[...], preferred_element_type=F32)

    lane_h = lax.broadcasted_iota(jnp.int32, (nb, LANE), 1)
    f_lane = lane_h < half
    h_r = jnp.zeros((nb, LANE), F32)
    h_i = jnp.zeros((nb, LANE), F32)
    for j in range(n_chunks):
        cf, cb = j, order_bwd[j]
        hf_sc[cf * nb:(cf + 1) * nb, 0:LANE] = h_r
        hf_sc[cf * nb:(cf + 1) * nb, LANE:2 * LANE] = h_i
        hb_sc[cb * nb:(cb + 1) * nb, 0:LANE] = h_r
        hb_sc[cb * nb:(cb + 1) * nb, LANE:2 * LANE] = h_i
        s_r = jnp.where(f_lane, s_sc[cf * nb:(cf + 1) * nb, 0:LANE], s_sc[cb * nb:(cb + 1) * nb, 0:LANE])
        s_i = jnp.where(f_lane, s_sc[cf * nb:(cf + 1) * nb, LANE:2 * LANE],
                        s_sc[cb * nb:(cb + 1) * nb, LANE:2 * LANE])
        h_r, h_i = tr * h_r - ti * h_i + s_r, tr * h_i + ti * h_r + s_i
    lane_2 = lax.broadcasted_iota(jnp.int32, hf_sc.shape, 1)
    h_in = jnp.where((lane_2 % LANE) < half, hf_sc[...], hb_sc[...])
    h_hi = h_in.astype(BF16)
    h_lo = (h_in - h_hi.astype(F32)).astype(BF16)
    wout = wout_sc[...]
    y = (jnp.dot(lhs, wtoep_sc[...], preferred_element_type=F32)
         + jnp.dot(h_hi, wout, preferred_element_type=F32)
         + jnp.dot(h_lo, wout, preferred_element_type=F32))
    y_t = jnp.stack([y[:, o * t_len:(o + 1) * t_len] for o in range(S5_GROUP)], axis=0)
    y_ref[...] = jnp.swapaxes(y_t, 0, 1)


def _s5_out_kernel(yt_ref, g_ref, x_ref, mod_ref, wglu_ref, bglu_ref, wout_ref, lng_ref, lnb_ref, out_ref):
    parts = [jnp.transpose(_gelu_tanh(yt_ref[s, 0])).astype(BF16) for s in range(yt_ref.shape[0])]
    a = jnp.concatenate(parts, axis=0)
    z = jnp.dot(a, wglu_ref[...], preferred_element_type=F32) + bglu_ref[...]
    t = z[:, :E_BRANCH] * _sigmoid(z[:, E_BRANCH:]) * _silu(g_ref[0].astype(F32))
    y = jnp.dot(t.astype(BF16), wout_ref[...], preferred_element_type=F32)
    out_ref[0] = _residual_norm(x_ref, mod_ref, y, lng_ref, lnb_ref)


def _s5_layer(xt, mod, n_ctx, w_in, lam_re, lam_im, log_dt, b_re, b_im, c_re, c_im, d_skip, w_glu, b_glu, w_out,
              ln_g, ln_b):
    bsz, nt, d = xt.shape
    tm = TOK_TILE
    e, ng, gs, ns, tl = E_BRANCH, S5_GROUPS, S5_GROUP, S5_STATE, S5_CHUNK
    nct, ntile = n_ctx // tm, nt // tm
    nsub = tm // tl
    nch, nch_ctx = nt // tl, n_ctx // tl
    cb = nch * bsz
    assert n_ctx % tm == 0 and nt % tm == 0 and cb % 16 == 0 and bsz % 8 == 0

    tok = lambda last: pl.BlockSpec((1, tm, last), lambda b, t: (b, t, 0))
    ut, g = pl.pallas_call(
        _s5_in_kernel,
        grid=(bsz, ntile),
        in_specs=[tok(d), _mod_spec(nct), _const_spec((e, d)), _const_spec((d, e))],
        out_specs=[pl.BlockSpec((nsub, 1, e, tl), lambda b, t: (t, b, 0, 0)), tok(e)],
        out_shape=[jax.ShapeDtypeStruct((nch, bsz, e, tl), F32), jax.ShapeDtypeStruct((bsz, nt, e), BF16)],
        compiler_params=_params(("parallel", "parallel")),
        name="s5_in",
    )(xt, mod, jnp.transpose(w_in[:, :e]).astype(BF16), w_in[:, e:].astype(BF16))

    both = lambda a: jnp.concatenate([a[0], a[1]], axis=-1)
    lre, lim = both(lam_re), both(lam_im)
    ldt = jnp.repeat(jnp.transpose(log_dt), ns, axis=1)
    bt_r, bt_i = (both(jnp.swapaxes(a, 2, 3)) for a in (b_re, b_im))
    cr_r, cr_i = both(c_re), both(c_im)
    btil_r, btil_i = (jnp.tile(a, (1, gs, 1)) for a in (bt_r, bt_i))
    crep_r, crep_i = (jnp.repeat(a, gs, axis=1) for a in (cr_r, cr_i))
    ct_r, ct_i = (jnp.swapaxes(a, 1, 2) for a in (cr_r, cr_i))
    dterm = (d_skip.reshape(ng, gs, 1) * jnp.eye(gs, dtype=F32)).reshape(ng, gs * gs, 1)

    order_bwd = tuple(range(nch_ctx - 1, -1, -1)) + tuple(range(nch - 1, nch_ctx - 1, -1))
    row_spec = pl.BlockSpec((1, 1, 2 * ns), lambda gi: (gi, 0, 0))
    col_spec = pl.BlockSpec((1, 2 * ns, 1), lambda gi: (gi, 0, 0))
    g3 = lambda r, c: pl.BlockSpec((1, r, c), lambda gi: (gi, 0, 0))
    data_spec = pl.BlockSpec((cb, gs, tl), lambda gi: (0, gi, 0))
    yt = pl.pallas_call(
        functools.partial(_s5_scan_kernel, nb=bsz, order_bwd=order_bwd),
        grid=(ng,),
        in_specs=[data_spec, row_spec, row_spec, row_spec, col_spec, col_spec, col_spec,
                  g3(gs, 2 * ns), g3(gs, 2 * ns), g3(gs * gs, 2 * ns), g3(gs * gs, 2 * ns),
                  g3(gs * gs, 2 * ns), g3(gs * gs, 2 * ns), g3(2 * ns, gs), g3(2 * ns, gs), g3(gs * gs, 1)],
        out_specs=data_spec,
        out_shape=jax.ShapeDtypeStruct((cb, e, tl), F32),
        scratch_shapes=[pltpu.VMEM((cb, gs * tl), BF16), pltpu.VMEM((gs * tl, 4 * ns), BF16),
                        pltpu.VMEM((4 * ns, gs * tl), BF16), pltpu.VMEM((gs * tl, gs * tl), BF16),
                        pltpu.VMEM((gs * gs, 2 * tl), F32), pltpu.VMEM((cb, 4 * ns), F32),
                        pltpu.VMEM((cb, 4 * ns), F32), pltpu.VMEM((cb, 4 * ns), F32)],
        compiler_params=_params(("parallel",)),
        name="s5_scan",
    )(ut.reshape(cb, e, tl), lre[:, None, :], lim[:, None, :], ldt[:, None, :],
      lre[:, :, None], lim[:, :, None], ldt[:, :, None],
      bt_r, bt_i, btil_r, btil_i, crep_r, crep_i, ct_r, ct_i, dterm)

    return pl.pallas_call(
        _s5_out_kernel,
        grid=(bsz, ntile),
        in_specs=[pl.BlockSpec((nsub, 1, e, tl), lambda b, t: (t, b, 0, 0)), tok(e), tok(d), _mod_spec(nct),
                  _const_spec((e, 2 * e)), _const_spec((1, 2 * e)), _const_spec((e, d)),
                  _const_spec((1, d)), _const_spec((1, d))],
        out_specs=tok(d),
        out_shape=jax.ShapeDtypeStruct((bsz, nt, d), F32),
        compiler_params=_params(("parallel", "parallel")),
        name="s5_out",
    )(yt.reshape(nch, bsz, e, tl), g, xt, mod, w_glu.astype(BF16), b_glu.reshape(1, -1), w_out.astype(BF16),
      ln_g.reshape(1, d), ln_b.reshape(1, d))


def _conv_in_kernel(x_ref, mod_ref, w_ref, v_ref, g_ref):
    h = _modulate(x_ref, mod_ref).astype(BF16)
    z = jnp.dot(h, w_ref[...], preferred_element_type=F32)
    v_ref[0] = (z[:, :E_BRANCH] * _sigmoid(z[:, E_BRANCH:2 * E_BRANCH])).astype(BF16)
    g_ref[0] = z[:, 2 * E_BRANCH:].astype(BF16)


def _conv_out_kernel(v_ref, vp_ref, vn_ref, g_ref, x_ref, mod_ref, dw_ref, dwb_ref, cg_ref, cb_ref, wout_ref,
                     lng_ref, lnb_ref, out_ref, vext_sc, y_sc, *, nct, ntile, row_block):
    tm = v_ref.shape[1]
    t = pl.program_id(1)
    has_prev = jnp.logical_and(t != 0, t != nct)
    has_next = jnp.logical_and(t != nct - 1, t != ntile - 1)
    vext_sc[0:CONV_HALO] = jnp.where(has_prev, vp_ref[0].astype(F32), 0.0)
    vext_sc[CONV_HALO:CONV_HALO + tm] = v_ref[0].astype(F32)
    vext_sc[CONV_HALO + tm:2 * CONV_HALO + tm] = jnp.where(has_next, vn_ref[0].astype(F32), 0.0)

    shift0 = CONV_HALO - CONV_PAD

    def lane_block(cbk, carry):
        lo = pl.multiple_of(cbk * LANE, LANE)
        w = dw_ref[:, pl.ds(lo, LANE)]
        bias = dwb_ref[:, pl.ds(lo, LANE)]
        for rb in range(tm // row_block):
            acc = jnp.zeros((row_block, LANE), F32)
            for rho in range(SUBLANE):
                z = None
                for k in range(CONV_WIDTH):
                    if (k + shift0) % SUBLANE != rho:
                        continue
                    start = rb * row_block + (k + shift0) - rho
                    term = vext_sc[pl.ds(start, row_block + SUBLANE), pl.ds(lo, LANE)] * w[k:k + 1]
                    z = term if z is None else z + term
                if z is not None:
                    acc = acc + z[rho:rho + row_block]
            y_sc[rb * row_block:(rb + 1) * row_block, pl.ds(lo, LANE)] = acc + bias
        return carry
    lax.fori_loop(0, E_BRANCH // LANE, lane_block, 0)

    c = _silu(_layernorm(y_sc[...], cg_ref[...], cb_ref[...]))
    y = jnp.dot((c * _silu(g_ref[0].astype(F32))).astype(BF16), wout_ref[...], preferred_element_type=F32)
    out_ref[0] = _residual_norm(x_ref, mod_ref, y, lng_ref, lnb_ref)


def _conv_layer(xt, mod, n_ctx, w_in, dw, dw_b, cln_g, cln_b, w_out, ln_g, ln_b):
    bsz, nt, d = xt.shape
    tm, e = TOK_TILE, E_BRANCH
    nct, ntile = n_ctx // tm, nt // tm
    assert n_ctx % tm == 0 and nt % tm == 0
    tok = lambda last: pl.BlockSpec((1, tm, last), lambda b, t: (b, t, 0))
    v, g = pl.pallas_call(
        _conv_in_kernel,
        grid=(bsz, ntile),
        in_specs=[tok(d), _mod_spec(nct), _const_spec((d, 3 * e))],
        out_specs=[tok(e), tok(e)],
        out_shape=[jax.ShapeDtypeStruct((bsz, nt, e), BF16), jax.ShapeDtypeStruct((bsz, nt, e), BF16)],
        compiler_params=_params(("parallel", "parallel")),
        name="conv_in",
    )(xt, mod, w_in.astype(BF16))

    hb = tm // CONV_HALO
    nhb = nt // CONV_HALO
    dw_pad = jnp.concatenate([dw, jnp.zeros((32 - CONV_WIDTH, e), F32)], axis=0)
    return pl.pallas_call(
        functools.partial(_conv_out_kernel, nct=nct, ntile=ntile, row_block=128),
        grid=(bsz, ntile),
        in_specs=[tok(e),
                  pl.BlockSpec((1, CONV_HALO, e), lambda b, t: (b, jnp.maximum(t * hb - 1, 0), 0)),
                  pl.BlockSpec((1, CONV_HALO, e), lambda b, t: (b, jnp.minimum((t + 1) * hb, nhb - 1), 0)),
                  tok(e), tok(d), _mod_spec(nct),
                  _const_spec((32, e)), _const_spec((1, e)), _const_spec((1, e)), _const_spec((1, e)),
                  _const_spec((e, d)), _const_spec((1, d)), _const_spec((1, d))],
        out_specs=tok(d),
        out_shape=jax.ShapeDtypeStruct((bsz, nt, d), F32),
        scratch_shapes=[pltpu.VMEM((tm + 2 * CONV_HALO, e), F32), pltpu.VMEM((tm, e), F32)],
        compiler_params=_params(("parallel", "parallel")),
        name="conv_out",
    )(v, v, v, g, xt, mod, dw_pad, dw_b.reshape(1, e), cln_g.reshape(1, e), cln_b.reshape(1, e),
      w_out.astype(BF16), ln_g.reshape(1, d), ln_b.reshape(1, d))


def kernel(x, c, ctx, c_ctx, w_mod, b_mod, ln_g, ln_b, mla_w_in, mla_q_norm, mla_kv_norm, mla_w_uq, mla_w_uk, mla_w_uv, mla_w_out, s5_w_in, s5_lam_re, s5_lam_im, s5_log_dt, s5_b_re, s5_b_im, s5_c_re, s5_c_im, s5_d, s5_w_glu, s5_b_glu, s5_w_out, cv_w_in, cv_dw, cv_dw_b, cv_ln_g, cv_ln_b, cv_w_out):
    n_ctx, n_lat = ctx.shape[1], x.shape[1]
    mod = _modulation(c, c_ctx, w_mod, b_mod)
    tables = _rope_tables(n_ctx, n_lat)
    xt = jnp.concatenate([ctx, x], axis=1)
    for i in range(DEPTH):
        kind, j = i % N_MIXERS, i // N_MIXERS
        if kind == 0:
            xt = _mla_layer(xt, mod[i], tables, n_ctx, mla_w_in[j], mla_q_norm[j], mla_kv_norm[j], mla_w_uq[j],
                            mla_w_uk[j], mla_w_uv[j], mla_w_out[j], ln_g[i], ln_b[i], i < DEPTH - 1)
        elif kind == 1:
            xt = _s5_layer(xt, mod[i], n_ctx, s5_w_in[j], s5_lam_re[j], s5_lam_im[j], s5_log_dt[j], s5_b_re[j],
                           s5_b_im[j], s5_c_re[j], s5_c_im[j], s5_d[j], s5_w_glu[j], s5_b_glu[j], s5_w_out[j],
                           ln_g[i], ln_b[i])
        else:
            xt = _conv_layer(xt, mod[i], n_ctx, cv_w_in[j], cv_dw[j], cv_dw_b[j], cv_ln_g[j], cv_ln_b[j],
                             cv_w_out[j], ln_g[i], ln_b[i])
    return xt
```

```python
import functools
import math

import jax
import jax.numpy as jnp
from jax import lax
from jax.experimental import pallas as pl
from jax.experimental.pallas import tpu as pltpu

F32 = jnp.float32
BF16 = jnp.bfloat16

D_MODEL = 1024
DEPTH = 4
N_MIXERS = 3
GRID_W = 64
E_BRANCH = 2 * D_MODEL
MLA_HEADS = 16
MLA_NOPE = 128
MLA_ROPE = 64
MLA_V = 128
MLA_Q_RANK = 256
MLA_KV_RANK = 128
MLA_KEY = 256
MLA_QCOLS = 384
ROPE_BASE = 10000.0
S5_GROUP = 16
S5_GROUPS = E_BRANCH // S5_GROUP
S5_STATE = 64
S5_CHUNK = 128
CONV_WIDTH = 31
CONV_PAD = CONV_WIDTH // 2
CONV_HALO = 16
DEEPNORM_ALPHA = (2.0 * DEPTH) ** 0.25
NORM_EPS = 1e-6
LANE = 128
SUBLANE = 8
VMEM_LIMIT = 56 * 1024 * 1024

TOK_TILE = 256
ATT_TQ = 128
ATT_TK = 512
ATT_ROW_BLOCK = MLA_HEADS * ATT_TQ
MOD_ROWS = 24


def _sigmoid(x):
    return 1.0 / (1.0 + jnp.exp(-x))


def _silu(x):
    return x * _sigmoid(x)


def _gelu_tanh(x):
    return 0.5 * x * (1.0 + jnp.tanh(math.sqrt(2.0 / math.pi) * (x + 0.044715 * (x * x * x))))


def _layernorm(r, g, b):
    mu = jnp.mean(r, axis=-1, keepdims=True)
    d = r - mu
    var = jnp.mean(d * d, axis=-1, keepdims=True)
    return d * lax.rsqrt(var + NORM_EPS) * g + b


def _rmsnorm(x, g):
    return x * lax.rsqrt(jnp.mean(x * x, axis=-1, keepdims=True) + NORM_EPS) * g


def _params(sem):
    return pltpu.CompilerParams(dimension_semantics=sem, vmem_limit_bytes=VMEM_LIMIT)


def _const_spec(shape):
    nd = len(shape)
    return pl.BlockSpec(shape, lambda *_: (0,) * nd, pipeline_mode=pl.Buffered(1))


def _mod_kernel(c_ref, w_ref, b_ref, o_ref):
    c = c_ref[...]
    o_ref[0] = jnp.dot(_silu(c), w_ref[0], preferred_element_type=F32,
                       precision=lax.Precision.HIGHEST) + b_ref[0]


def _modulation(c, c_ctx, w_mod, b_mod):
    bsz, d = c.shape
    assert bsz + 1 <= MOD_ROWS
    rows = jnp.concatenate([c, c_ctx[None], jnp.zeros((MOD_ROWS - bsz - 1, d), F32)], axis=0)
    out = pl.pallas_call(
        _mod_kernel,
        grid=(DEPTH, 3),
        in_specs=[pl.BlockSpec((MOD_ROWS, d), lambda i, j: (0, 0)),
                  pl.BlockSpec((1, d, d), lambda i, j: (i, 0, j)),
                  pl.BlockSpec((1, 1, d), lambda i, j: (i, 0, j))],
        out_specs=pl.BlockSpec((1, MOD_ROWS, d), lambda i, j: (i, 0, j)),
        out_shape=jax.ShapeDtypeStruct((DEPTH, MOD_ROWS, 3 * d), F32),
        compiler_params=_params(("parallel", "parallel")),
        name="modulation",
    )(rows, w_mod, b_mod.reshape(DEPTH, 1, 3 * d))
    out = out.reshape(DEPTH, MOD_ROWS, 3, d)
    lat = out[:, :bsz]
    ctx = jnp.broadcast_to(out[:, bsz][:, None], lat.shape)
    return jnp.stack([ctx, lat], axis=2)


def _mod_spec(nct):
    return pl.BlockSpec((1, 1, 3, D_MODEL), lambda b, t: (b, jnp.where(t < nct, 0, 1), 0, 0))


def _stream_specs(stream, tm, nct, t_off=0):
    d = stream[0].shape[-1]
    if len(stream) == 1:
        return [pl.BlockSpec((1, tm, d), lambda b, t: (b, t + t_off, 0))]
    return [pl.BlockSpec((1, tm, d), lambda b, t: (b, jnp.minimum(t + t_off, nct - 1), 0)),
            pl.BlockSpec((1, tm, d), lambda b, t: (b, jnp.maximum(t + t_off - nct, 0), 0))]


def _stream_tile(x_refs, nct, t_off=0):
    if len(x_refs) == 1:
        return x_refs[0][0]
    return jnp.where(pl.program_id(1) + t_off < nct, x_refs[0][0], x_refs[1][0])


def _modulate(x, mod_ref):
    return x * (1.0 + mod_ref[0, 0, 1:2, :]) + mod_ref[0, 0, 0:1, :]


def _residual_norm(x, mod_ref, y, lng_ref, lnb_ref):
    r = DEEPNORM_ALPHA * x + mod_ref[0, 0, 2:3, :] * y
    return _layernorm(r, lng_ref[...], lnb_ref[...])


def _rot_cols(w):
    x1, x2, x3, x4 = jnp.split(w, 4, axis=-1)
    return jnp.concatenate([-x2, x1, -x4, x3], axis=-1)


def _fold_kernel(uq_ref, uk_ref, o_ref):
    o_ref[0] = jnp.dot(uq_ref[0], uk_ref[0], preferred_element_type=F32, precision=lax.Precision.HIGHEST)


def _mla_in_kernel(*refs, n_x, nct, tq, qscale):
    x_refs = refs[:n_x]
    mod_ref, cq_ref, sq_ref, ck_ref, w1_ref, w2_ref, qn_ref, kvn_ref, q_ref, k_ref, v_ref, g_ref = refs[n_x:]
    tm = x_refs[0].shape[1]
    h = _modulate(_stream_tile(x_refs, nct), mod_ref).astype(BF16)
    z = jnp.dot(h, w1_ref[...], preferred_element_type=F32)
    qn = _rmsnorm(z[:, 0:MLA_Q_RANK], qn_ref[...]).astype(BF16)
    ckv = _rmsnorm(z[:, 256:384], kvn_ref[...]).astype(BF16)
    sq = sq_ref[...]
    kk = z[:, 384:512] * ck_ref[...] + z[:, 512:640] * sq
    k_ref[0, :, 0:LANE] = ckv
    k_ref[0, :, LANE:2 * LANE] = kk.astype(BF16)
    v_ref[0, :, 0:LANE] = ckv
    lane = lax.broadcasted_iota(jnp.int32, (tm, LANE), 1)
    v_ref[0, :, LANE:2 * LANE] = jnp.where(lane == 0, 1.0, 0.0).astype(BF16)
    g_ref[0] = z[:, 640:].astype(BF16)
    qq = jnp.dot(qn, w2_ref[...], preferred_element_type=F32)
    cq = cq_ref[...]
    for hh in range(MLA_HEADS):
        base = hh * MLA_QCOLS
        qa = qq[:, base:base + LANE] * qscale
        qr = (qq[:, base + LANE:base + 2 * LANE] * cq + qq[:, base + 2 * LANE:base + 3 * LANE] * sq) * qscale
        for s in range(tm // tq):
            q_ref[0, s, hh * tq:(hh + 1) * tq, 0:LANE] = qa[s * tq:(s + 1) * tq].astype(BF16)
            q_ref[0, s, hh * tq:(hh + 1) * tq, LANE:2 * LANE] = qr[s * tq:(s + 1) * tq].astype(BF16)


def _attn_kernel(q_ref, k_ref, v_ref, o_ref, m_sc, acc_sc, sa_sc, sb_sc, *, tq, tk, q_off, nct_q, n_ctx, n_lat_chunks):
    q = q_ref[0, 0]

    rows = q.shape[0]

    def scores(buf, off, size):
        kc = k_ref[0, pl.ds(off, size), :]
        buf[:, 0:size] = lax.dot_general(q, kc, (((1,), (1,)), ((), ())), preferred_element_type=F32)

    def update(buf, off, size, first=False):
        vc = v_ref[0, pl.ds(off, size), :]
        for r0 in range(0, rows, ATT_ROW_BLOCK):
            rs = slice(r0, r0 + ATT_ROW_BLOCK)
            s = buf[rs, 0:size]
            m_cur = jnp.max(s, axis=-1, keepdims=True)
            m_new = jnp.broadcast_to(m_cur, (ATT_ROW_BLOCK, LANE)) if first else jnp.maximum(m_sc[rs], m_cur)
            p = jnp.concatenate([jnp.exp2(s[:, j * LANE:(j + 1) * LANE] - m_new).astype(BF16)
                                 for j in range(size // LANE)], axis=1)
            pv = jnp.dot(p, vc, preferred_element_type=F32)
            if first:
                acc_sc[rs] = pv
            else:
                alpha = jnp.exp2(m_sc[rs] - m_new)
                acc_sc[rs, 0:LANE] = alpha * acc_sc[rs, 0:LANE] + pv[:, 0:LANE]
                acc_sc[rs, LANE:2 * LANE] = alpha * acc_sc[rs, LANE:2 * LANE] + pv[:, LANE:2 * LANE]
            m_sc[rs] = m_new

    def lat_off(c):
        return pl.multiple_of(n_ctx + c * tk, LANE)

    def latent_chunks():
        n = n_lat_chunks
        scores(sa_sc, n_ctx, tk)
        if n == 1:
            update(sa_sc, n_ctx, tk)
            return

        def pair(i, carry):
            scores(sb_sc, lat_off(2 * i + 1), tk)
            update(sa_sc, lat_off(2 * i), tk)
            scores(sa_sc, lat_off(2 * i + 2), tk)
            update(sb_sc, lat_off(2 * i + 1), tk)
            return carry

        npairs = n // 2
        if npairs > 1:
            lax.fori_loop(0, npairs - 1, pair, 0)
        last = 2 * (npairs - 1)
        scores(sb_sc, n_ctx + (last + 1) * tk, tk)
        update(sa_sc, n_ctx + last * tk, tk)
        if n % 2:
            scores(sa_sc, n_ctx + (last + 2) * tk, tk)
        update(sb_sc, n_ctx + (last + 1) * tk, tk)
        if n % 2:
            update(sa_sc, n_ctx + (last + 2) * tk, tk)

    scores(sb_sc, 0, n_ctx)
    update(sb_sc, 0, n_ctx, first=True)
    if q_off >= nct_q:
        latent_chunks()
    else:
        pl.when(pl.program_id(1) + q_off >= nct_q)(latent_chunks)

    o = acc_sc[:, 0:LANE] * (1.0 / acc_sc[:, LANE:LANE + 1])
    for hh in range(MLA_HEADS):
        o_ref[0, :, hh * LANE:(hh + 1) * LANE] = o[hh * tq:(hh + 1) * tq].astype(BF16)


def _mla_out_kernel(*refs, n_x, nct, t_off):
    x_refs = refs[:n_x]
    o_ref, g_ref, mod_ref, wuv_ref, wout_ref, lng_ref, lnb_ref, out_ref = refs[n_x:]
    parts = []
    for j in range(MLA_HEADS // 2):
        cols = slice(j * 2 * LANE, (j + 1) * 2 * LANE)
        v = jnp.dot(o_ref[0, :, cols], wuv_ref[j], preferred_element_type=F32)
        parts.append(v.astype(BF16) * _silu(g_ref[0, :, cols]))
    vg = jnp.concatenate(parts, axis=1)
    y = jnp.dot(vg, wout_ref[...], preferred_element_type=F32)
    out_ref[0] = _residual_norm(_stream_tile(x_refs, nct, t_off), mod_ref, y, lng_ref, lnb_ref)


def _mla_layer(stream, mod, tables, n_ctx, w_in, q_norm, kv_norm, w_uq, w_uk, w_uv, w_out, ln_g, ln_b, with_ctx_out):
    bsz, _, d = stream[-1].shape
    nt = sum(a.shape[1] for a in stream)
    assert len(stream) == 1 or with_ctx_out
    tm, tq = TOK_TILE, ATT_TQ
    tk = ATT_TK
    assert n_ctx % tm == 0 and nt % tm == 0 and n_ctx <= ATT_TK and (nt - n_ctx) % tk == 0
    nct = n_ctx // tm
    ntile = nt // tm
    hdim = MLA_HEADS
    qscale = (MLA_NOPE + MLA_ROPE) ** -0.5 * math.log2(math.e)

    kr = w_in[:, 384:448]
    w1 = jnp.concatenate([w_in[:, 0:384], kr, kr, _rot_cols(kr), jnp.zeros_like(kr), w_in[:, 448:]],
                         axis=1).astype(BF16)
    uq = w_uq.reshape(MLA_Q_RANK, hdim, MLA_NOPE + MLA_ROPE)
    fold = pl.pallas_call(
        _fold_kernel,
        grid=(hdim,),
        in_specs=[pl.BlockSpec((1, MLA_Q_RANK, MLA_NOPE), lambda h: (h, 0, 0)),
                  pl.BlockSpec((1, MLA_NOPE, MLA_KV_RANK), lambda h: (h, 0, 0))],
        out_specs=pl.BlockSpec((1, MLA_Q_RANK, MLA_KV_RANK), lambda h: (h, 0, 0)),
        out_shape=jax.ShapeDtypeStruct((hdim, MLA_Q_RANK, MLA_KV_RANK), F32),
        compiler_params=_params(("parallel",)),
        name="mla_fold",
    )(jnp.transpose(uq[:, :, :MLA_NOPE], (1, 0, 2)), jnp.transpose(w_uk, (1, 2, 0)))
    rp = uq[:, :, MLA_NOPE:]
    w2 = jnp.concatenate([jnp.transpose(fold, (1, 0, 2)), rp, rp, _rot_cols(rp), jnp.zeros_like(rp)],
                         axis=2).reshape(MLA_Q_RANK, hdim * MLA_QCOLS).astype(BF16)
    wuv = jnp.transpose(w_uv, (1, 0, 2))
    zero = jnp.zeros_like(wuv[0::2])
    wuv2 = jnp.concatenate([jnp.concatenate([wuv[0::2], zero], axis=2),
                            jnp.concatenate([zero, wuv[1::2]], axis=2)], axis=1).astype(BF16)
    cq, sq, ck = tables

    tok = lambda last: pl.BlockSpec((1, tm, last), lambda b, t: (b, t, 0))
    tab = pl.BlockSpec((tm, LANE), lambda b, t: (t, 0))
    q, k, v, g = pl.pallas_call(
        functools.partial(_mla_in_kernel, n_x=len(stream), nct=nct, tq=tq, qscale=qscale),
        grid=(bsz, ntile),
        in_specs=_stream_specs(stream, tm, nct) + [_mod_spec(nct), tab, tab, tab,
                  _const_spec(w1.shape), _const_spec(w2.shape),
                  _const_spec((1, MLA_Q_RANK)), _const_spec((1, MLA_KV_RANK))],
        out_specs=[pl.BlockSpec((1, tm // tq, hdim * tq, MLA_KEY), lambda b, t: (b, t, 0, 0)),
                   tok(MLA_KEY), tok(MLA_KEY), tok(hdim * MLA_V)],
        out_shape=[jax.ShapeDtypeStruct((bsz, nt // tq, hdim * tq, MLA_KEY), BF16),
                   jax.ShapeDtypeStruct((bsz, nt, MLA_KEY), BF16),
                   jax.ShapeDtypeStruct((bsz, nt, MLA_KEY), BF16),
                   jax.ShapeDtypeStruct((bsz, nt, hdim * MLA_V), BF16)],
        compiler_params=_params(("parallel", "parallel")),
        name="mla_in",
    )(*stream, mod, cq, sq, ck, w1, w2, q_norm.reshape(1, -1), kv_norm.reshape(1, -1))

    q_off = 0 if with_ctx_out else n_ctx // tq
    nq = nt // tq - q_off
    rows = hdim * tq
    o = pl.pallas_call(
        functools.partial(_attn_kernel, tq=tq, tk=tk, q_off=q_off, nct_q=n_ctx // tq, n_ctx=n_ctx,
                          n_lat_chunks=(nt - n_ctx) // tk),
        grid=(bsz, nq),
        in_specs=[pl.BlockSpec((1, 1, rows, MLA_KEY), lambda b, t: (b, t + q_off, 0, 0)),
                  pl.BlockSpec((1, nt, MLA_KEY), lambda b, t: (b, 0, 0)),
                  pl.BlockSpec((1, nt, MLA_KEY), lambda b, t: (b, 0, 0))],
        out_specs=pl.BlockSpec((1, tq, hdim * MLA_V), lambda b, t: (b, t, 0)),
        out_shape=jax.ShapeDtypeStruct((bsz, nq * tq, hdim * MLA_V), BF16),
        scratch_shapes=[pltpu.VMEM((rows, LANE), F32), pltpu.VMEM((rows, 2 * LANE), F32),
                        pltpu.VMEM((rows, tk), F32), pltpu.VMEM((rows, tk), F32)],
        compiler_params=_params(("parallel", "parallel")),
        name="mla_attention",
    )(q, k, v)

    t_off = 0 if with_ctx_out else nct
    n_out = ntile - t_off
    sh = lambda last: pl.BlockSpec((1, tm, last), lambda b, t: (b, t + t_off, 0))
    return pl.pallas_call(
        functools.partial(_mla_out_kernel, n_x=len(stream), nct=nct, t_off=t_off),
        grid=(bsz, n_out),
        in_specs=_stream_specs(stream, tm, nct, t_off) + [
                  tok(hdim * MLA_V), sh(hdim * MLA_V),
                  pl.BlockSpec((1, 1, 3, d), lambda b, t: (b, jnp.where(t + t_off < nct, 0, 1), 0, 0)),
                  _const_spec(wuv2.shape), _const_spec((hdim * MLA_V, d)),
                  _const_spec((1, d)), _const_spec((1, d))],
        out_specs=tok(d),
        out_shape=jax.ShapeDtypeStruct((bsz, n_out * tm, d), F32),
        compiler_params=_params(("parallel", "parallel")),
        name="mla_out",
    )(*stream, o, g, mod, wuv2, w_out.astype(BF16), ln_g.reshape(1, d), ln_b.reshape(1, d))


def _rope_tables(n_ctx, n_lat):
    rows = n_lat // GRID_W
    r, col = jnp.meshgrid(jnp.arange(rows, dtype=F32), jnp.arange(GRID_W, dtype=F32), indexing="ij")
    quarter = MLA_ROPE // 4
    inv = ROPE_BASE ** (-jnp.arange(quarter, dtype=F32) / quarter)
    ang_r = r.reshape(-1)[:, None] * inv
    ang_c = col.reshape(-1)[:, None] * inv
    ang = jnp.concatenate([ang_r, ang_r, ang_c, ang_c], axis=-1)
    cos, sin = jnp.cos(ang), jnp.sin(ang)
    one, zero = jnp.ones_like(cos), jnp.zeros_like(cos)
    czero, cone = jnp.zeros((n_ctx, MLA_ROPE), F32), jnp.ones((n_ctx, MLA_ROPE), F32)
    cq = jnp.concatenate([jnp.concatenate([czero, cone], 1), jnp.concatenate([cos, one], 1)], 0)
    sq = jnp.concatenate([jnp.concatenate([czero, czero], 1), jnp.concatenate([sin, zero], 1)], 0)
    ck = jnp.concatenate([jnp.concatenate([czero, cone], 1), jnp.concatenate([cos, zero], 1)], 0)
    return cq, sq, ck


def _s5_in_kernel(x_ref, mod_ref, wut_ref, wg_ref, ut_ref, g_ref):
    h = _modulate(x_ref[0], mod_ref).astype(BF16)
    g_ref[0] = jnp.dot(h, wg_ref[...], preferred_element_type=F32).astype(BF16)
    ut = lax.dot_general(wut_ref[...], h, (((1,), (1,)), ((), ())), preferred_element_type=F32)
    for s in range(ut_ref.shape[0]):
        ut_ref[s, 0] = ut[:, s * S5_CHUNK:(s + 1) * S5_CHUNK]


def _cexp(scale, ar, ai):
    mag = jnp.exp(scale * ar)
    ang = scale * ai
    return mag * jnp.cos(ang), mag * jnp.sin(ang)


def _s5_scan_kernel(u_ref, lre_r, lim_r, ldt_r, lre_c, lim_c, ldt_c, btr_ref, bti_ref, btilr_ref, btili_ref,
                    crepr_ref, crepi_ref, ctr_ref, cti_ref, dterm_ref, y_ref,
                    lhs_sc, wst_sc, wout_sc, wtoep_sc, v_sc, s_sc, hf_sc, hb_sc, *, nb, order_bwd):
    t_len = S5_CHUNK
    half = S5_STATE
    hi = lax.Precision.HIGHEST
    n_chunks = len(order_bwd)

    ar = jnp.minimum(lre_r[0], -1e-4) * jnp.exp(ldt_r[0])
    ai = lim_r[0] * jnp.exp(ldt_r[0])
    lr, li = jnp.minimum(lre_r[0], -1e-4), lim_r[0]
    lbr, lbi = _cexp(1.0, ar, ai)
    den = lr * lr + li * li
    fr = ((lbr - 1.0) * lr + lbi * li) / den
    fi = (lbi * lr - (lbr - 1.0) * li) / den
    tr, ti = _cexp(float(t_len), ar, ai)

    row = lax.broadcasted_iota(jnp.int32, (t_len, LANE), 0)
    lane = lax.broadcasted_iota(jnp.int32, (t_len, LANE), 1)
    e_s = jnp.where(lane < half, t_len - 1 - row, row).astype(F32)
    es_r, es_i = _cexp(e_s, ar, ai)
    bbr = btr_ref[0] * fr - bti_ref[0] * fi
    bbi = btr_ref[0] * fi + bti_ref[0] * fr
    for i in range(S5_GROUP):
        br, bi = bbr[i:i + 1], bbi[i:i + 1]
        wst_sc[i * t_len:(i + 1) * t_len, 0:LANE] = (es_r * br - es_i * bi).astype(BF16)
        wst_sc[i * t_len:(i + 1) * t_len, LANE:2 * LANE] = (es_r * bi + es_i * br).astype(BF16)

    dt_c = jnp.exp(ldt_c[0])
    ar_c = jnp.minimum(lre_c[0], -1e-4) * dt_c
    ai_c = lim_c[0] * dt_c
    is_f = row < half
    e_k = jnp.where(is_f, lane, t_len - lane).astype(F32)
    pk_r, pk_i = _cexp(e_k, ar_c, ai_c)
    lbr_c, lbi_c = _cexp(1.0, ar_c, ai_c)
    po_r = jnp.where(is_f, pk_r * lbr_c - pk_i * lbi_c, pk_r)
    po_i = jnp.where(is_f, pk_r * lbi_c + pk_i * lbr_c, pk_i)
    ctr, cti = ctr_ref[0], cti_ref[0]
    for o in range(S5_GROUP):
        cr, ci = ctr[:, o:o + 1], cti[:, o:o + 1]
        wout_sc[0:LANE, o * t_len:(o + 1) * t_len] = (cr * po_r - ci * po_i).astype(BF16)
        wout_sc[LANE:2 * LANE, o * t_len:(o + 1) * t_len] = (-(cr * po_i + ci * po_r)).astype(BF16)

    mbr = btilr_ref[0] * fr - btili_ref[0] * fi
    mbi = btilr_ref[0] * fi + btili_ref[0] * fr
    m_r = crepr_ref[0] * mbr - crepi_ref[0] * mbi
    m_i = crepr_ref[0] * mbi + crepi_ref[0] * mbr
    zero = jnp.zeros_like(pk_r)
    pcat_r = jnp.concatenate([jnp.where(is_f, zero, pk_r), jnp.where(is_f, pk_r, zero)], axis=1)
    pcat_i = jnp.concatenate([jnp.where(is_f, zero, pk_i), jnp.where(is_f, pk_i, zero)], axis=1)
    v = (jnp.dot(m_r, pcat_r, preferred_element_type=F32, precision=hi)
         - jnp.dot(m_i, pcat_i, preferred_element_type=F32, precision=hi))
    lane_m = lax.broadcasted_iota(jnp.int32, m_r.shape, 1)
    kb0 = jnp.sum(jnp.where(lane_m >= half, m_r, 0.0), axis=1, keepdims=True)
    lane_v = lax.broadcasted_iota(jnp.int32, v.shape, 1)
    v_sc[...] = v + jnp.where(lane_v == t_len, kb0 + dterm_ref[0], 0.0)

    def toep_o(o, carry):
        vblk = v_sc[pl.ds(pl.multiple_of(o * S5_GROUP, S5_GROUP), S5_GROUP), :]
        col = pl.ds(pl.multiple_of(o * t_len, t_len), t_len)
        for i in range(S5_GROUP):
            row_b = jnp.broadcast_to(vblk[i:i + 1], (t_len, 2 * t_len))
            rolled = pltpu.roll(row_b, 0, 1, stride=1, stride_axis=0)
            wtoep_sc[i * t_len:(i + 1) * t_len, col] = rolled[:, t_len:].astype(BF16)
        return carry
    lax.fori_loop(0, S5_GROUP, toep_o, 0)

    u_t = jnp.swapaxes(u_ref[...], 0, 1)
    for i in range(S5_GROUP):
        lhs_sc[:, i * t_len:(i + 1) * t_len] = u_t[i].astype(BF16)
    lhs = lhs_sc[...]
    s_sc[...] = jnp.dot(lhs, wst_sc[...], preferred_element_type=F32)

    lane_h = lax.broadcasted_iota(jnp.int32, (nb, LANE), 1)
    f_lane = lane_h < half
    h_r = jnp.zeros((nb, LANE), F32)
    h_i = jnp.zeros((nb, LANE), F32)
    for j in range(n_chunks):
        cf, cb = j, order_bwd[j]
        hf_sc[cf * nb:(cf + 1) * nb, 0:LANE] = h_r
        hf_sc[cf * nb:(cf + 1) * nb, LANE:2 * LANE] = h_i
        hb_sc[cb * nb:(cb + 1) * nb, 0:LANE] = h_r
        hb_sc[cb * nb:(cb + 1) * nb, LANE:2 * LANE] = h_i
        s_r = jnp.where(f_lane, s_sc[cf * nb:(cf + 1) * nb, 0:LANE], s_sc[cb * nb:(cb + 1) * nb, 0:LANE])
        s_i = jnp.where(f_lane, s_sc[cf * nb:(cf + 1) * nb, LANE:2 * LANE],
                        s_sc[cb * nb:(cb + 1) * nb, LANE:2 * LANE])
        h_r, h_i = tr * h_r - ti * h_i + s_r, tr * h_i + ti * h_r + s_i
    lane_2 = lax.broadcasted_iota(jnp.int32, hf_sc.shape, 1)
    h_in = jnp.where((lane_2 % LANE) < half, hf_sc[...], hb_sc[...])
    h_hi = h_in.astype(BF16)
    h_lo = (h_in - h_hi.astype(F32)).astype(BF16)
    wout = wout_sc[...]
    y = (jnp.dot(lhs, wtoep_sc[...], preferred_element_type=F32)
         + jnp.dot(h_hi, wout, preferred_element_type=F32)
         + jnp.dot(h_lo, wout, preferred_element_type=F32))
    y_t = jnp.stack([y[:, o * t_len:(o + 1) * t_len] for o in range(S5_GROUP)], axis=0)
    y_ref[...] = jnp.swapaxes(y_t, 0, 1)


def _s5_out_kernel(yt_ref, g_ref, x_ref, mod_ref, wglu_ref, bglu_ref, wout_ref, lng_ref, lnb_ref, out_ref):
    parts = [jnp.transpose(_gelu_tanh(yt_ref[s, 0])).astype(BF16) for s in range(yt_ref.shape[0])]
    a = jnp.concatenate(parts, axis=0)
    z = jnp.dot(a, wglu_ref[...], preferred_element_type=F32) + bglu_ref[...]
    t = (z[:, :E_BRANCH] * _sigmoid(z[:, E_BRANCH:])).astype(BF16) * _silu(g_ref[0])
    y = jnp.dot(t, wout_ref[...], preferred_element_type=F32)
    out_ref[0] = _residual_norm(x_ref[0], mod_ref, y, lng_ref, lnb_ref)


def _s5_layer(xt, mod, n_ctx, w_in, lam_re, lam_im, log_dt, b_re, b_im, c_re, c_im, d_skip, w_glu, b_glu, w_out,
              ln_g, ln_b):
    bsz, nt, d = xt.shape
    tm = TOK_TILE
    e, ng, gs, ns, tl = E_BRANCH, S5_GROUPS, S5_GROUP, S5_STATE, S5_CHUNK
    nct, ntile = n_ctx // tm, nt // tm
    nsub = tm // tl
    nch, nch_ctx = nt // tl, n_ctx // tl
    cb = nch * bsz
    assert n_ctx % tm == 0 and nt % tm == 0 and cb % 16 == 0 and bsz % 8 == 0

    tok = lambda last: pl.BlockSpec((1, tm, last), lambda b, t: (b, t, 0))
    ut, g = pl.pallas_call(
        _s5_in_kernel,
        grid=(bsz, ntile),
        in_specs=[tok(d), _mod_spec(nct), _const_spec((e, d)), _const_spec((d, e))],
        out_specs=[pl.BlockSpec((nsub, 1, e, tl), lambda b, t: (t, b, 0, 0)), tok(e)],
        out_shape=[jax.ShapeDtypeStruct((nch, bsz, e, tl), F32), jax.ShapeDtypeStruct((bsz, nt, e), BF16)],
        compiler_params=_params(("parallel", "parallel")),
        name="s5_in",
    )(xt, mod, jnp.transpose(w_in[:, :e]).astype(BF16), w_in[:, e:].astype(BF16))

    both = lambda a: jnp.concatenate([a[0], a[1]], axis=-1)
    lre, lim = both(lam_re), both(lam_im)
    ldt = jnp.repeat(jnp.transpose(log_dt), ns, axis=1)
    bt_r, bt_i = (both(jnp.swapaxes(a, 2, 3)) for a in (b_re, b_im))
    cr_r, cr_i = both(c_re), both(c_im)
    btil_r, btil_i = (jnp.tile(a, (1, gs, 1)) for a in (bt_r, bt_i))
    crep_r, crep_i = (jnp.repeat(a, gs, axis=1) for a in (cr_r, cr_i))
    ct_r, ct_i = (jnp.swapaxes(a, 1, 2) for a in (cr_r, cr_i))
    dterm = (d_skip.reshape(ng, gs, 1) * jnp.eye(gs, dtype=F32)).reshape(ng, gs * gs, 1)

    order_bwd = tuple(range(nch_ctx - 1, -1, -1)) + tuple(range(nch - 1, nch_ctx - 1, -1))
    row_spec = pl.BlockSpec((1, 1, 2 * ns), lambda gi: (gi, 0, 0))
    col_spec = pl.BlockSpec((1, 2 * ns, 1), lambda gi: (gi, 0, 0))
    g3 = lambda r, c: pl.BlockSpec((1, r, c), lambda gi: (gi, 0, 0))
    data_spec = pl.BlockSpec((cb, gs, tl), lambda gi: (0, gi, 0))
    yt = pl.pallas_call(
        functools.partial(_s5_scan_kernel, nb=bsz, order_bwd=order_bwd),
        grid=(ng,),
        in_specs=[data_spec, row_spec, row_spec, row_spec, col_spec, col_spec, col_spec,
                  g3(gs, 2 * ns), g3(gs, 2 * ns), g3(gs * gs, 2 * ns), g3(gs * gs, 2 * ns),
                  g3(gs * gs, 2 * ns), g3(gs * gs, 2 * ns), g3(2 * ns, gs), g3(2 * ns, gs), g3(gs * gs, 1)],
        out_specs=data_spec,
        out_shape=jax.ShapeDtypeStruct((cb, e, tl), F32),
        scratch_shapes=[pltpu.VMEM((cb, gs * tl), BF16), pltpu.VMEM((gs * tl, 4 * ns), BF16),
                        pltpu.VMEM((4 * ns, gs * tl), BF16), pltpu.VMEM((gs * tl, gs * tl), BF16),
                        pltpu.VMEM((gs * gs, 2 * tl), F32), pltpu.VMEM((cb, 4 * ns), F32),
                        pltpu.VMEM((cb, 4 * ns), F32), pltpu.VMEM((cb, 4 * ns), F32)],
        compiler_params=_params(("parallel",)),
        name="s5_scan",
    )(ut.reshape(cb, e, tl), lre[:, None, :], lim[:, None, :], ldt[:, None, :],
      lre[:, :, None], lim[:, :, None], ldt[:, :, None],
      bt_r, bt_i, btil_r, btil_i, crep_r, crep_i, ct_r, ct_i, dterm)

    return pl.pallas_call(
        _s5_out_kernel,
        grid=(bsz, ntile),
        in_specs=[pl.BlockSpec((nsub, 1, e, tl), lambda b, t: (t, b, 0, 0)), tok(e), tok(d), _mod_spec(nct),
                  _const_spec((e, 2 * e)), _const_spec((1, 2 * e)), _const_spec((e, d)),
                  _const_spec((1, d)), _const_spec((1, d))],
        out_specs=tok(d),
        out_shape=jax.ShapeDtypeStruct((bsz, nt, d), F32),
        compiler_params=_params(("parallel", "parallel")),
        name="s5_out",
    )(yt.reshape(nch, bsz, e, tl), g, xt, mod, w_glu.astype(BF16), b_glu.reshape(1, -1), w_out.astype(BF16),
      ln_g.reshape(1, d), ln_b.reshape(1, d))


def _conv_in_kernel(x_ref, mod_ref, w_ref, v_ref, g_ref):
    h = _modulate(x_ref[0], mod_ref).astype(BF16)
    z = jnp.dot(h, w_ref[...], preferred_element_type=F32)
    v_ref[0] = (z[:, :E_BRANCH] * _sigmoid(z[:, E_BRANCH:2 * E_BRANCH])).astype(BF16)
    g_ref[0] = z[:, 2 * E_BRANCH:].astype(BF16)


def _conv_out_kernel(v_ref, vp_ref, vn_ref, g_ref, x_ref, mod_ref, dw_ref, dwb_ref, cg_ref, cb_ref, wout_ref,
                     lng_ref, lnb_ref, out_ref, vext_sc, y_sc, *, nct, ntile, row_block):
    tm = v_ref.shape[1]
    t = pl.program_id(1)
    has_prev = jnp.logical_and(t != 0, t != nct)
    has_next = jnp.logical_and(t != nct - 1, t != ntile - 1)
    vext_sc[0:CONV_HALO] = jnp.where(has_prev, vp_ref[0].astype(F32), 0.0)
    vext_sc[CONV_HALO:CONV_HALO + tm] = v_ref[0].astype(F32)
    vext_sc[CONV_HALO + tm:2 * CONV_HALO + tm] = jnp.where(has_next, vn_ref[0].astype(F32), 0.0)

    shift0 = CONV_HALO - CONV_PAD

    def lane_block(cbk, carry):
        lo = pl.multiple_of(cbk * LANE, LANE)
        w = dw_ref[:, pl.ds(lo, LANE)]
        bias = dwb_ref[:, pl.ds(lo, LANE)]
        for rb in range(tm // row_block):
            acc = jnp.zeros((row_block, LANE), F32)
            for rho in range(SUBLANE):
                z = None
                for k in range(CONV_WIDTH):
                    if (k + shift0) % SUBLANE != rho:
                        continue
                    start = rb * row_block + (k + shift0) - rho
                    term = vext_sc[pl.ds(start, row_block + SUBLANE), pl.ds(lo, LANE)] * w[k:k + 1]
                    z = term if z is None else z + term
                if z is not None:
                    acc = acc + z[rho:rho + row_block]
            y_sc[rb * row_block:(rb + 1) * row_block, pl.ds(lo, LANE)] = acc + bias
        return carry
    lax.fori_loop(0, E_BRANCH // LANE, lane_block, 0)

    c = _silu(_layernorm(y_sc[...], cg_ref[...], cb_ref[...]))
    y = jnp.dot(c.astype(BF16) * _silu(g_ref[0]), wout_ref[...], preferred_element_type=F32)
    out_ref[0] = _residual_norm(x_ref[0], mod_ref, y, lng_ref, lnb_ref)


def _conv_layer(xt, mod, n_ctx, w_in, dw, dw_b, cln_g, cln_b, w_out, ln_g, ln_b):
    bsz, nt, d = xt.shape
    tm, e = TOK_TILE, E_BRANCH
    nct, ntile = n_ctx // tm, nt // tm
    assert n_ctx % tm == 0 and nt % tm == 0
    tok = lambda last: pl.BlockSpec((1, tm, last), lambda b, t: (b, t, 0))
    v, g = pl.pallas_call(
        _conv_in_kernel,
        grid=(bsz, ntile),
        in_specs=[tok(d), _mod_spec(nct), _const_spec((d, 3 * e))],
        out_specs=[tok(e), tok(e)],
        out_shape=[jax.ShapeDtypeStruct((bsz, nt, e), BF16), jax.ShapeDtypeStruct((bsz, nt, e), BF16)],
        compiler_params=_params(("parallel", "parallel")),
        name="conv_in",
    )(xt, mod, w_in.astype(BF16))

    hb = tm // CONV_HALO
    nhb = nt // CONV_HALO
    dw_pad = jnp.concatenate([dw, jnp.zeros((32 - CONV_WIDTH, e), F32)], axis=0)
    return pl.pallas_call(
        functools.partial(_conv_out_kernel, nct=nct, ntile=ntile, row_block=128),
        grid=(bsz, ntile),
        in_specs=[tok(e),
                  pl.BlockSpec((1, CONV_HALO, e), lambda b, t: (b, jnp.maximum(t * hb - 1, 0), 0)),
                  pl.BlockSpec((1, CONV_HALO, e), lambda b, t: (b, jnp.minimum((t + 1) * hb, nhb - 1), 0)),
                  tok(e), tok(d), _mod_spec(nct),
                  _const_spec((32, e)), _const_spec((1, e)), _const_spec((1, e)), _const_spec((1, e)),
                  _const_spec((e, d)), _const_spec((1, d)), _const_spec((1, d))],
        out_specs=tok(d),
        out_shape=jax.ShapeDtypeStruct((bsz, nt, d), F32),
        scratch_shapes=[pltpu.VMEM((tm + 2 * CONV_HALO, e), F32), pltpu.VMEM((tm, e), F32)],
        compiler_params=_params(("parallel", "parallel")),
        name="conv_out",
    )(v, v, v, g, xt, mod, dw_pad, dw_b.reshape(1, e), cln_g.reshape(1, e), cln_b.reshape(1, e),
      w_out.astype(BF16), ln_g.reshape(1, d), ln_b.reshape(1, d))


def kernel(x, c, ctx, c_ctx, w_mod, b_mod, ln_g, ln_b, mla_w_in, mla_q_norm, mla_kv_norm, mla_w_uq, mla_w_uk, mla_w_uv, mla_w_out, s5_w_in, s5_lam_re, s5_lam_im, s5_log_dt, s5_b_re, s5_b_im, s5_c_re, s5_c_im, s5_d, s5_w_glu, s5_b_glu, s5_w_out, cv_w_in, cv_dw, cv_dw_b, cv_ln_g, cv_ln_b, cv_w_out):
    n_ctx, n_lat = ctx.shape[1], x.shape[1]
    mod = _modulation(c, c_ctx, w_mod, b_mod)
    tables = _rope_tables(n_ctx, n_lat)
    xt = None
    for i in range(DEPTH):
        kind, j = i % N_MIXERS, i // N_MIXERS
        if kind == 0:
            stream = (ctx, x) if i == 0 else (xt,)
            xt = _mla_layer(stream, mod[i], tables, n_ctx, mla_w_in[j], mla_q_norm[j], mla_kv_norm[j], mla_w_uq[j],
                            mla_w_uk[j], mla_w_uv[j], mla_w_out[j], ln_g[i], ln_b[i], i < DEPTH - 1)
        elif kind == 1:
            xt = _s5_layer(xt, mod[i], n_ctx, s5_w_in[j], s5_lam_re[j], s5_lam_im[j], s5_log_dt[j], s5_b_re[j],
                           s5_b_im[j], s5_c_re[j], s5_c_im[j], s5_d[j], s5_w_glu[j], s5_b_glu[j], s5_w_out[j],
                           ln_g[i], ln_b[i])
        else:
            xt = _conv_layer(xt, mod[i], n_ctx, cv_w_in[j], cv_dw[j], cv_dw_b[j], cv_ln_g[j], cv_ln_b[j],
                             cv_w_out[j], ln_g[i], ln_b[i])
    return xt
```

```python
import functools
import math

import jax
import jax.numpy as jnp
from jax import lax
from jax.experimental import pallas as pl
from jax.experimental.pallas import tpu as pltpu

F32 = jnp.float32
BF16 = jnp.bfloat16

D_MODEL = 1024
DEPTH = 4
N_MIXERS = 3
GRID_W = 64
E_BRANCH = 2 * D_MODEL
MLA_HEADS = 16
MLA_NOPE = 128
MLA_ROPE = 64
MLA_V = 128
MLA_Q_RANK = 256
MLA_KV_RANK = 128
MLA_KEY = 256
MLA_QCOLS = 384
ROPE_BASE = 10000.0
S5_GROUP = 16
S5_GROUPS = E_BRANCH // S5_GROUP
S5_STATE = 64
S5_CHUNK = 128
CONV_WIDTH = 31
CONV_PAD = CONV_WIDTH // 2
CONV_HALO = 16
DEEPNORM_ALPHA = (2.0 * DEPTH) ** 0.25
NORM_EPS = 1e-6
LANE = 128
SUBLANE = 8
BF16_ROWS = 16
VMEM_LIMIT = 56 * 1024 * 1024

TOK_TILE = 256
ATT_TQ = 128
ATT_TK = 512
ATT_ROW_BLOCK = MLA_HEADS * ATT_TQ
MOD_ROWS = 24


def _sigmoid(x):
    return 1.0 / (1.0 + jnp.exp(-x))


def _silu(x):
    return x * _sigmoid(x)


def _gelu_tanh(x):
    return 0.5 * x * (1.0 + jnp.tanh(math.sqrt(2.0 / math.pi) * (x + 0.044715 * (x * x * x))))


def _layernorm(r, g, b):
    mu = jnp.mean(r, axis=-1, keepdims=True)
    d = r - mu
    var = jnp.mean(d * d, axis=-1, keepdims=True)
    return d * lax.rsqrt(var + NORM_EPS) * g + b


def _rmsnorm(x, g):
    return x * lax.rsqrt(jnp.mean(x * x, axis=-1, keepdims=True) + NORM_EPS) * g


def _params(sem):
    return pltpu.CompilerParams(dimension_semantics=sem, vmem_limit_bytes=VMEM_LIMIT)


def _const_spec(shape):
    nd = len(shape)
    return pl.BlockSpec(shape, lambda *_: (0,) * nd, pipeline_mode=pl.Buffered(1))


def _mod_kernel(c_ref, w_ref, b_ref, o_ref):
    c = c_ref[...]
    o_ref[0] = jnp.dot(_silu(c), w_ref[0], preferred_element_type=F32,
                       precision=lax.Precision.HIGHEST) + b_ref[0]


def _modulation(c, c_ctx, w_mod, b_mod):
    bsz, d = c.shape
    assert bsz + 1 <= MOD_ROWS
    rows = jnp.concatenate([c, c_ctx[None], jnp.zeros((MOD_ROWS - bsz - 1, d), F32)], axis=0)
    out = pl.pallas_call(
        _mod_kernel,
        grid=(DEPTH, 3),
        in_specs=[pl.BlockSpec((MOD_ROWS, d), lambda i, j: (0, 0)),
                  pl.BlockSpec((1, d, d), lambda i, j: (i, 0, j)),
                  pl.BlockSpec((1, 1, d), lambda i, j: (i, 0, j))],
        out_specs=pl.BlockSpec((1, MOD_ROWS, d), lambda i, j: (i, 0, j)),
        out_shape=jax.ShapeDtypeStruct((DEPTH, MOD_ROWS, 3 * d), F32),
        compiler_params=_params(("parallel", "parallel")),
        name="modulation",
    )(rows, w_mod, b_mod.reshape(DEPTH, 1, 3 * d))
    out = out.reshape(DEPTH, MOD_ROWS, 3, d)
    lat = out[:, :bsz]
    ctx = jnp.broadcast_to(out[:, bsz][:, None], lat.shape)
    return jnp.stack([ctx, lat], axis=2)


def _mod_spec(nct):
    return pl.BlockSpec((1, 1, 3, D_MODEL), lambda b, t: (b, jnp.where(t < nct, 0, 1), 0, 0))


def _stream_specs(stream, tm, nct, t_off=0):
    d = stream[0].shape[-1]
    if len(stream) == 1:
        return [pl.BlockSpec((1, tm, d), lambda b, t: (b, t + t_off, 0))]
    return [pl.BlockSpec((1, tm, d), lambda b, t: (b, jnp.minimum(t + t_off, nct - 1), 0)),
            pl.BlockSpec((1, tm, d), lambda b, t: (b, jnp.maximum(t + t_off - nct, 0), 0))]


def _stream_tile(x_refs, nct, t_off=0):
    if len(x_refs) == 1:
        return x_refs[0][0]
    return jnp.where(pl.program_id(1) + t_off < nct, x_refs[0][0], x_refs[1][0])


def _modulate(x, mod_ref):
    return x * (1.0 + mod_ref[0, 0, 1:2, :]) + mod_ref[0, 0, 0:1, :]


def _residual_norm(x, mod_ref, y, lng_ref, lnb_ref):
    r = DEEPNORM_ALPHA * x + mod_ref[0, 0, 2:3, :] * y
    return _layernorm(r, lng_ref[...], lnb_ref[...])


def _rot_cols(w):
    x1, x2, x3, x4 = jnp.split(w, 4, axis=-1)
    return jnp.concatenate([-x2, x1, -x4, x3], axis=-1)


def _fold_kernel(uq_ref, uk_ref, o_ref):
    o_ref[0] = jnp.dot(uq_ref[0], uk_ref[0], preferred_element_type=F32, precision=lax.Precision.HIGHEST)


def _mla_in_kernel(*refs, n_x, nct, tq, qscale):
    x_refs = refs[:n_x]
    mod_ref, cq_ref, sq_ref, ck_ref, w1_ref, w2_ref, qn_ref, kvn_ref, q_ref, k_ref, v_ref, g_ref = refs[n_x:]
    tm = x_refs[0].shape[1]
    h = _modulate(_stream_tile(x_refs, nct), mod_ref).astype(BF16)
    z = jnp.dot(h, w1_ref[...], preferred_element_type=F32)
    qn = _rmsnorm(z[:, 0:MLA_Q_RANK], qn_ref[...]).astype(BF16)
    ckv = _rmsnorm(z[:, 256:384], kvn_ref[...]).astype(BF16)
    sq = sq_ref[...]
    kk = z[:, 384:512] * ck_ref[...] + z[:, 512:640] * sq
    k_ref[0, :, 0:LANE] = ckv
    k_ref[0, :, LANE:2 * LANE] = kk.astype(BF16)
    v_ref[0, :, 0:LANE] = ckv
    lane = lax.broadcasted_iota(jnp.int32, (tm, LANE), 1)
    v_ref[0, :, LANE:2 * LANE] = jnp.where(lane == 0, 1.0, 0.0).astype(BF16)
    g_ref[0] = z[:, 640:].astype(BF16)
    qq = jnp.dot(qn, w2_ref[...], preferred_element_type=F32)
    cq = cq_ref[...]
    for hh in range(MLA_HEADS):
        base = hh * MLA_QCOLS
        qa = qq[:, base:base + LANE] * qscale
        qr = (qq[:, base + LANE:base + 2 * LANE] * cq + qq[:, base + 2 * LANE:base + 3 * LANE] * sq) * qscale
        for s in range(tm // tq):
            q_ref[0, s, hh * tq:(hh + 1) * tq, 0:LANE] = qa[s * tq:(s + 1) * tq].astype(BF16)
            q_ref[0, s, hh * tq:(hh + 1) * tq, LANE:2 * LANE] = qr[s * tq:(s + 1) * tq].astype(BF16)


def _attn_kernel(q_ref, k_ref, v_ref, o_ref, m_sc, acc_sc, sa_sc, sb_sc, *, tq, tk, q_off, nct_q, n_ctx, n_lat_chunks):
    q = q_ref[0, 0]

    rows = q.shape[0]

    def scores(buf, off, size):
        kc = k_ref[0, pl.ds(off, size), :]
        buf[:, 0:size] = lax.dot_general(q, kc, (((1,), (1,)), ((), ())), preferred_element_type=F32)

    def update(buf, off, size, first=False):
        vc = v_ref[0, pl.ds(off, size), :]
        for r0 in range(0, rows, ATT_ROW_BLOCK):
            rs = slice(r0, r0 + ATT_ROW_BLOCK)
            s = buf[rs, 0:size]
            m_cur = jnp.max(s, axis=-1, keepdims=True)
            m_new = jnp.broadcast_to(m_cur, (ATT_ROW_BLOCK, LANE)) if first else jnp.maximum(m_sc[rs], m_cur)
            p = jnp.concatenate([jnp.exp2(s[:, j * LANE:(j + 1) * LANE] - m_new).astype(BF16)
                                 for j in range(size // LANE)], axis=1)
            pv = jnp.dot(p, vc, preferred_element_type=F32)
            if first:
                acc_sc[rs] = pv
            else:
                alpha = jnp.exp2(m_sc[rs] - m_new)
                acc_sc[rs, 0:LANE] = alpha * acc_sc[rs, 0:LANE] + pv[:, 0:LANE]
                acc_sc[rs, LANE:2 * LANE] = alpha * acc_sc[rs, LANE:2 * LANE] + pv[:, LANE:2 * LANE]
            m_sc[rs] = m_new

    def lat_off(c):
        return pl.multiple_of(n_ctx + c * tk, LANE)

    def latent_chunks():
        n = n_lat_chunks
        scores(sa_sc, n_ctx, tk)
        if n == 1:
            update(sa_sc, n_ctx, tk)
            return

        def pair(i, carry):
            scores(sb_sc, lat_off(2 * i + 1), tk)
            update(sa_sc, lat_off(2 * i), tk)
            scores(sa_sc, lat_off(2 * i + 2), tk)
            update(sb_sc, lat_off(2 * i + 1), tk)
            return carry

        npairs = n // 2
        if npairs > 1:
            lax.fori_loop(0, npairs - 1, pair, 0)
        last = 2 * (npairs - 1)
        scores(sb_sc, n_ctx + (last + 1) * tk, tk)
        update(sa_sc, n_ctx + last * tk, tk)
        if n % 2:
            scores(sa_sc, n_ctx + (last + 2) * tk, tk)
        update(sb_sc, n_ctx + (last + 1) * tk, tk)
        if n % 2:
            update(sa_sc, n_ctx + (last + 2) * tk, tk)

    scores(sb_sc, 0, n_ctx)
    update(sb_sc, 0, n_ctx, first=True)
    if q_off >= nct_q:
        latent_chunks()
    else:
        pl.when(pl.program_id(1) + q_off >= nct_q)(latent_chunks)

    o = acc_sc[:, 0:LANE] * (1.0 / acc_sc[:, LANE:LANE + 1])
    for hh in range(MLA_HEADS):
        o_ref[0, :, hh * LANE:(hh + 1) * LANE] = o[hh * tq:(hh + 1) * tq].astype(BF16)


def _mla_out_kernel(*refs, n_x, nct, t_off):
    x_refs = refs[:n_x]
    o_ref, g_ref, mod_ref, wuv_ref, wout_ref, lng_ref, lnb_ref, out_ref = refs[n_x:]
    parts = []
    for j in range(MLA_HEADS // 2):
        cols = slice(j * 2 * LANE, (j + 1) * 2 * LANE)
        v = jnp.dot(o_ref[0, :, cols], wuv_ref[j], preferred_element_type=F32)
        parts.append(v.astype(BF16) * _silu(g_ref[0, :, cols]))
    vg = jnp.concatenate(parts, axis=1)
    y = jnp.dot(vg, wout_ref[...], preferred_element_type=F32)
    out_ref[0] = _residual_norm(_stream_tile(x_refs, nct, t_off), mod_ref, y, lng_ref, lnb_ref)


def _mla_layer(stream, mod, tables, n_ctx, w_in, q_norm, kv_norm, w_uq, w_uk, w_uv, w_out, ln_g, ln_b, with_ctx_out):
    bsz, _, d = stream[-1].shape
    nt = sum(a.shape[1] for a in stream)
    assert len(stream) == 1 or with_ctx_out
    tm, tq = TOK_TILE, ATT_TQ
    tk = ATT_TK
    assert n_ctx % tm == 0 and nt % tm == 0 and n_ctx <= ATT_TK and (nt - n_ctx) % tk == 0
    nct = n_ctx // tm
    ntile = nt // tm
    hdim = MLA_HEADS
    qscale = (MLA_NOPE + MLA_ROPE) ** -0.5 * math.log2(math.e)

    kr = w_in[:, 384:448]
    w1 = jnp.concatenate([w_in[:, 0:384], kr, kr, _rot_cols(kr), jnp.zeros_like(kr), w_in[:, 448:]],
                         axis=1).astype(BF16)
    uq = w_uq.reshape(MLA_Q_RANK, hdim, MLA_NOPE + MLA_ROPE)
    fold = pl.pallas_call(
        _fold_kernel,
        grid=(hdim,),
        in_specs=[pl.BlockSpec((1, MLA_Q_RANK, MLA_NOPE), lambda h: (h, 0, 0)),
                  pl.BlockSpec((1, MLA_NOPE, MLA_KV_RANK), lambda h: (h, 0, 0))],
        out_specs=pl.BlockSpec((1, MLA_Q_RANK, MLA_KV_RANK), lambda h: (h, 0, 0)),
        out_shape=jax.ShapeDtypeStruct((hdim, MLA_Q_RANK, MLA_KV_RANK), F32),
        compiler_params=_params(("parallel",)),
        name="mla_fold",
    )(jnp.transpose(uq[:, :, :MLA_NOPE], (1, 0, 2)), jnp.transpose(w_uk, (1, 2, 0)))
    rp = uq[:, :, MLA_NOPE:]
    w2 = jnp.concatenate([jnp.transpose(fold, (1, 0, 2)), rp, rp, _rot_cols(rp), jnp.zeros_like(rp)],
                         axis=2).reshape(MLA_Q_RANK, hdim * MLA_QCOLS).astype(BF16)
    wuv = jnp.transpose(w_uv, (1, 0, 2))
    zero = jnp.zeros_like(wuv[0::2])
    wuv2 = jnp.concatenate([jnp.concatenate([wuv[0::2], zero], axis=2),
                            jnp.concatenate([zero, wuv[1::2]], axis=2)], axis=1).astype(BF16)
    cq, sq, ck = tables

    tok = lambda last: pl.BlockSpec((1, tm, last), lambda b, t: (b, t, 0))
    tab = pl.BlockSpec((tm, LANE), lambda b, t: (t, 0))
    q, k, v, g = pl.pallas_call(
        functools.partial(_mla_in_kernel, n_x=len(stream), nct=nct, tq=tq, qscale=qscale),
        grid=(bsz, ntile),
        in_specs=_stream_specs(stream, tm, nct) + [_mod_spec(nct), tab, tab, tab,
                  _const_spec(w1.shape), _const_spec(w2.shape),
                  _const_spec((1, MLA_Q_RANK)), _const_spec((1, MLA_KV_RANK))],
        out_specs=[pl.BlockSpec((1, tm // tq, hdim * tq, MLA_KEY), lambda b, t: (b, t, 0, 0)),
                   tok(MLA_KEY), tok(MLA_KEY), tok(hdim * MLA_V)],
        out_shape=[jax.ShapeDtypeStruct((bsz, nt // tq, hdim * tq, MLA_KEY), BF16),
                   jax.ShapeDtypeStruct((bsz, nt, MLA_KEY), BF16),
                   jax.ShapeDtypeStruct((bsz, nt, MLA_KEY), BF16),
                   jax.ShapeDtypeStruct((bsz, nt, hdim * MLA_V), BF16)],
        compiler_params=_params(("parallel", "parallel")),
        name="mla_in",
    )(*stream, mod, cq, sq, ck, w1, w2, q_norm.reshape(1, -1), kv_norm.reshape(1, -1))

    q_off = 0 if with_ctx_out else n_ctx // tq
    nq = nt // tq - q_off
    rows = hdim * tq
    o = pl.pallas_call(
        functools.partial(_attn_kernel, tq=tq, tk=tk, q_off=q_off, nct_q=n_ctx // tq, n_ctx=n_ctx,
                          n_lat_chunks=(nt - n_ctx) // tk),
        grid=(bsz, nq),
        in_specs=[pl.BlockSpec((1, 1, rows, MLA_KEY), lambda b, t: (b, t + q_off, 0, 0)),
                  pl.BlockSpec((1, nt, MLA_KEY), lambda b, t: (b, 0, 0)),
                  pl.BlockSpec((1, nt, MLA_KEY), lambda b, t: (b, 0, 0))],
        out_specs=pl.BlockSpec((1, tq, hdim * MLA_V), lambda b, t: (b, t, 0)),
        out_shape=jax.ShapeDtypeStruct((bsz, nq * tq, hdim * MLA_V), BF16),
        scratch_shapes=[pltpu.VMEM((rows, LANE), F32), pltpu.VMEM((rows, 2 * LANE), F32),
                        pltpu.VMEM((rows, tk), F32), pltpu.VMEM((rows, tk), F32)],
        compiler_params=_params(("parallel", "parallel")),
        name="mla_attention",
    )(q, k, v)

    t_off = 0 if with_ctx_out else nct
    n_out = ntile - t_off
    sh = lambda last: pl.BlockSpec((1, tm, last), lambda b, t: (b, t + t_off, 0))
    return pl.pallas_call(
        functools.partial(_mla_out_kernel, n_x=len(stream), nct=nct, t_off=t_off),
        grid=(bsz, n_out),
        in_specs=_stream_specs(stream, tm, nct, t_off) + [
                  tok(hdim * MLA_V), sh(hdim * MLA_V),
                  pl.BlockSpec((1, 1, 3, d), lambda b, t: (b, jnp.where(t + t_off < nct, 0, 1), 0, 0)),
                  _const_spec(wuv2.shape), _const_spec((hdim * MLA_V, d)),
                  _const_spec((1, d)), _const_spec((1, d))],
        out_specs=tok(d),
        out_shape=jax.ShapeDtypeStruct((bsz, n_out * tm, d), F32),
        compiler_params=_params(("parallel", "parallel")),
        name="mla_out",
    )(*stream, o, g, mod, wuv2, w_out.astype(BF16), ln_g.reshape(1, d), ln_b.reshape(1, d))


def _rope_tables(n_ctx, n_lat):
    rows = n_lat // GRID_W
    r, col = jnp.meshgrid(jnp.arange(rows, dtype=F32), jnp.arange(GRID_W, dtype=F32), indexing="ij")
    quarter = MLA_ROPE // 4
    inv = ROPE_BASE ** (-jnp.arange(quarter, dtype=F32) / quarter)
    ang_r = r.reshape(-1)[:, None] * inv
    ang_c = col.reshape(-1)[:, None] * inv
    ang = jnp.concatenate([ang_r, ang_r, ang_c, ang_c], axis=-1)
    cos, sin = jnp.cos(ang), jnp.sin(ang)
    one, zero = jnp.ones_like(cos), jnp.zeros_like(cos)
    czero, cone = jnp.zeros((n_ctx, MLA_ROPE), F32), jnp.ones((n_ctx, MLA_ROPE), F32)
    cq = jnp.concatenate([jnp.concatenate([czero, cone], 1), jnp.concatenate([cos, one], 1)], 0)
    sq = jnp.concatenate([jnp.concatenate([czero, czero], 1), jnp.concatenate([sin, zero], 1)], 0)
    ck = jnp.concatenate([jnp.concatenate([czero, cone], 1), jnp.concatenate([cos, zero], 1)], 0)
    return cq, sq, ck


def _s5_in_kernel(x_ref, mod_ref, wut_ref, wg_ref, ut_ref, g_ref):
    h = _modulate(x_ref[0], mod_ref).astype(BF16)
    g_ref[0] = jnp.dot(h, wg_ref[...], preferred_element_type=F32).astype(BF16)
    ut = lax.dot_general(wut_ref[...], h, (((1,), (1,)), ((), ())), preferred_element_type=F32)
    for s in range(ut_ref.shape[0]):
        ut_ref[s, 0] = ut[:, s * S5_CHUNK:(s + 1) * S5_CHUNK]


def _cexp(scale, ar, ai):
    mag = jnp.exp(scale * ar)
    ang = scale * ai
    return mag * jnp.cos(ang), mag * jnp.sin(ang)


def _s5_scan_kernel(u_ref, lre_r, lim_r, ldt_r, lre_c, lim_c, ldt_c, btr_ref, bti_ref, btilr_ref, btili_ref,
                    crepr_ref, crepi_ref, ctr_ref, cti_ref, dterm_ref, y_ref,
                    lhs_sc, wst_sc, wout_sc, wtoep_sc, v_sc, s_sc, hf_sc, hb_sc, *, nb, order_bwd):
    t_len = S5_CHUNK
    half = S5_STATE
    hi = lax.Precision.HIGHEST
    n_chunks = len(order_bwd)

    ar = jnp.minimum(lre_r[0], -1e-4) * jnp.exp(ldt_r[0])
    ai = lim_r[0] * jnp.exp(ldt_r[0])
    lr, li = jnp.minimum(lre_r[0], -1e-4), lim_r[0]
    lbr, lbi = _cexp(1.0, ar, ai)
    den = lr * lr + li * li
    fr = ((lbr - 1.0) * lr + lbi * li) / den
    fi = (lbi * lr - (lbr - 1.0) * li) / den
    tr, ti = _cexp(float(t_len), ar, ai)

    row = lax.broadcasted_iota(jnp.int32, (t_len, LANE), 0)
    lane = lax.broadcasted_iota(jnp.int32, (t_len, LANE), 1)
    e_s = jnp.where(lane < half, t_len - 1 - row, row).astype(F32)
    es_r, es_i = _cexp(e_s, ar, ai)
    bbr = btr_ref[0] * fr - bti_ref[0] * fi
    bbi = btr_ref[0] * fi + bti_ref[0] * fr
    for i in range(S5_GROUP):
        br, bi = bbr[i:i + 1], bbi[i:i + 1]
        wst_sc[i * t_len:(i + 1) * t_len, 0:LANE] = (es_r * br - es_i * bi).astype(BF16)
        wst_sc[i * t_len:(i + 1) * t_len, LANE:2 * LANE] = (es_r * bi + es_i * br).astype(BF16)

    dt_c = jnp.exp(ldt_c[0])
    ar_c = jnp.minimum(lre_c[0], -1e-4) * dt_c
    ai_c = lim_c[0] * dt_c
    is_f = row < half
    e_k = jnp.where(is_f, lane, t_len - lane).astype(F32)
    pk_r, pk_i = _cexp(e_k, ar_c, ai_c)
    lbr_c, lbi_c = _cexp(1.0, ar_c, ai_c)
    po_r = jnp.where(is_f, pk_r * lbr_c - pk_i * lbi_c, pk_r)
    po_i = jnp.where(is_f, pk_r * lbi_c + pk_i * lbr_c, pk_i)
    ctr, cti = ctr_ref[0], cti_ref[0]
    for o in range(S5_GROUP):
        cr, ci = ctr[:, o:o + 1], cti[:, o:o + 1]
        wout_sc[0:LANE, o * t_len:(o + 1) * t_len] = (cr * po_r - ci * po_i).astype(BF16)
        wout_sc[LANE:2 * LANE, o * t_len:(o + 1) * t_len] = (-(cr * po_i + ci * po_r)).astype(BF16)

    mbr = btilr_ref[0] * fr - btili_ref[0] * fi
    mbi = btilr_ref[0] * fi + btili_ref[0] * fr
    m_r = crepr_ref[0] * mbr - crepi_ref[0] * mbi
    m_i = crepr_ref[0] * mbi + crepi_ref[0] * mbr
    zero = jnp.zeros_like(pk_r)
    pcat_r = jnp.concatenate([jnp.where(is_f, zero, pk_r), jnp.where(is_f, pk_r, zero)], axis=1)
    pcat_i = jnp.concatenate([jnp.where(is_f, zero, pk_i), jnp.where(is_f, pk_i, zero)], axis=1)
    v = (jnp.dot(m_r, pcat_r, preferred_element_type=F32, precision=hi)
         - jnp.dot(m_i, pcat_i, preferred_element_type=F32, precision=hi))
    lane_m = lax.broadcasted_iota(jnp.int32, m_r.shape, 1)
    kb0 = jnp.sum(jnp.where(lane_m >= half, m_r, 0.0), axis=1, keepdims=True)
    lane_v = lax.broadcasted_iota(jnp.int32, v.shape, 1)
    v_sc[...] = v + jnp.where(lane_v == t_len, kb0 + dterm_ref[0], 0.0)

    for o in range(S5_GROUP):
        vblk = v_sc[o * S5_GROUP:(o + 1) * S5_GROUP, :]
        for i in range(S5_GROUP):
            row_b = jnp.broadcast_to(vblk[i:i + 1], (t_len, 2 * t_len))
            rolled = pltpu.roll(row_b, 0, 1, stride=1, stride_axis=0)
            wtoep_sc[i * t_len:(i + 1) * t_len, o * t_len:(o + 1) * t_len] = rolled[:, t_len:].astype(BF16)

    u_t = jnp.swapaxes(u_ref[...], 0, 1)
    for i in range(S5_GROUP):
        lhs_sc[:, i * t_len:(i + 1) * t_len] = u_t[i].astype(BF16)
    lhs = lhs_sc[...]
    s_sc[...] = jnp.dot(lhs, wst_sc[...], preferred_element_type=F32)

    lane_h = lax.broadcasted_iota(jnp.int32, (nb, LANE), 1)
    f_lane = lane_h < half
    h_r = jnp.zeros((nb, LANE), F32)
    h_i = jnp.zeros((nb, LANE), F32)
    for j in range(n_chunks):
        cf, cb = j, order_bwd[j]
        hf_sc[cf * nb:(cf + 1) * nb, 0:LANE] = h_r
        hf_sc[cf * nb:(cf + 1) * nb, LANE:2 * LANE] = h_i
        hb_sc[cb * nb:(cb + 1) * nb, 0:LANE] = h_r
        hb_sc[cb * nb:(cb + 1) * nb, LANE:2 * LANE] = h_i
        s_r = jnp.where(f_lane, s_sc[cf * nb:(cf + 1) * nb, 0:LANE], s_sc[cb * nb:(cb + 1) * nb, 0:LANE])
        s_i = jnp.where(f_lane, s_sc[cf * nb:(cf + 1) * nb, LANE:2 * LANE],
                        s_sc[cb * nb:(cb + 1) * nb, LANE:2 * LANE])
        h_r, h_i = tr * h_r - ti * h_i + s_r, tr * h_i + ti * h_r + s_i
    lane_2 = lax.broadcasted_iota(jnp.int32, hf_sc.shape, 1)
    h_in = jnp.where((lane_2 % LANE) < half, hf_sc[...], hb_sc[...])
    h_hi = h_in.astype(BF16)
    h_lo = (h_in - h_hi.astype(F32)).astype(BF16)
    wout = wout_sc[...]
    y = (jnp.dot(lhs, wtoep_sc[...], preferred_element_type=F32)
         + jnp.dot(h_hi, wout, preferred_element_type=F32)
         + jnp.dot(h_lo, wout, preferred_element_type=F32))
    y_t = jnp.stack([y[:, o * t_len:(o + 1) * t_len] for o in range(S5_GROUP)], axis=0)
    y_ref[...] = jnp.swapaxes(y_t, 0, 1)


def _s5_out_kernel(yt_ref, g_ref, x_ref, mod_ref, wglu_ref, bglu_ref, wout_ref, lng_ref, lnb_ref, out_ref):
    parts = [jnp.transpose(_gelu_tanh(yt_ref[s, 0])).astype(BF16) for s in range(yt_ref.shape[0])]
    a = jnp.concatenate(parts, axis=0)
    z = jnp.dot(a, wglu_ref[...], preferred_element_type=F32) + bglu_ref[...]
    t = (z[:, :E_BRANCH] * _sigmoid(z[:, E_BRANCH:])).astype(BF16) * _silu(g_ref[0])
    y = jnp.dot(t, wout_ref[...], preferred_element_type=F32)
    out_ref[0] = _residual_norm(x_ref[0], mod_ref, y, lng_ref, lnb_ref)


def _s5_layer(xt, mod, n_ctx, w_in, lam_re, lam_im, log_dt, b_re, b_im, c_re, c_im, d_skip, w_glu, b_glu, w_out,
              ln_g, ln_b):
    bsz, nt, d = xt.shape
    tm = TOK_TILE
    e, ng, gs, ns, tl = E_BRANCH, S5_GROUPS, S5_GROUP, S5_STATE, S5_CHUNK
    nct, ntile = n_ctx // tm, nt // tm
    nsub = tm // tl
    nch, nch_ctx = nt // tl, n_ctx // tl
    cb = nch * bsz
    assert n_ctx % tm == 0 and nt % tm == 0 and cb % 16 == 0 and bsz % 8 == 0

    tok = lambda last: pl.BlockSpec((1, tm, last), lambda b, t: (b, t, 0))
    ut, g = pl.pallas_call(
        _s5_in_kernel,
        grid=(bsz, ntile),
        in_specs=[tok(d), _mod_spec(nct), _const_spec((e, d)), _const_spec((d, e))],
        out_specs=[pl.BlockSpec((nsub, 1, e, tl), lambda b, t: (t, b, 0, 0)), tok(e)],
        out_shape=[jax.ShapeDtypeStruct((nch, bsz, e, tl), F32), jax.ShapeDtypeStruct((bsz, nt, e), BF16)],
        compiler_params=_params(("parallel", "parallel")),
        name="s5_in",
    )(xt, mod, jnp.transpose(w_in[:, :e]).astype(BF16), w_in[:, e:].astype(BF16))

    both = lambda a: jnp.concatenate([a[0], a[1]], axis=-1)
    lre, lim = both(lam_re), both(lam_im)
    ldt = jnp.repeat(jnp.transpose(log_dt), ns, axis=1)
    bt_r, bt_i = (both(jnp.swapaxes(a, 2, 3)) for a in (b_re, b_im))
    cr_r, cr_i = both(c_re), both(c_im)
    btil_r, btil_i = (jnp.tile(a, (1, gs, 1)) for a in (bt_r, bt_i))
    crep_r, crep_i = (jnp.repeat(a, gs, axis=1) for a in (cr_r, cr_i))
    ct_r, ct_i = (jnp.swapaxes(a, 1, 2) for a in (cr_r, cr_i))
    dterm = (d_skip.reshape(ng, gs, 1) * jnp.eye(gs, dtype=F32)).reshape(ng, gs * gs, 1)

    order_bwd = tuple(range(nch_ctx - 1, -1, -1)) + tuple(range(nch - 1, nch_ctx - 1, -1))
    row_spec = pl.BlockSpec((1, 1, 2 * ns), lambda gi: (gi, 0, 0))
    col_spec = pl.BlockSpec((1, 2 * ns, 1), lambda gi: (gi, 0, 0))
    g3 = lambda r, c: pl.BlockSpec((1, r, c), lambda gi: (gi, 0, 0))
    data_spec = pl.BlockSpec((cb, gs, tl), lambda gi: (0, gi, 0))
    yt = pl.pallas_call(
        functools.partial(_s5_scan_kernel, nb=bsz, order_bwd=order_bwd),
        grid=(ng,),
        in_specs=[data_spec, row_spec, row_spec, row_spec, col_spec, col_spec, col_spec,
                  g3(gs, 2 * ns), g3(gs, 2 * ns), g3(gs * gs, 2 * ns), g3(gs * gs, 2 * ns),
                  g3(gs * gs, 2 * ns), g3(gs * gs, 2 * ns), g3(2 * ns, gs), g3(2 * ns, gs), g3(gs * gs, 1)],
        out_specs=data_spec,
        out_shape=jax.ShapeDtypeStruct((cb, e, tl), F32),
        scratch_shapes=[pltpu.VMEM((cb, gs * tl), BF16), pltpu.VMEM((gs * tl, 4 * ns), BF16),
                        pltpu.VMEM((4 * ns, gs * tl), BF16), pltpu.VMEM((gs * tl, gs * tl), BF16),
                        pltpu.VMEM((gs * gs, 2 * tl), F32), pltpu.VMEM((cb, 4 * ns), F32),
                        pltpu.VMEM((cb, 4 * ns), F32), pltpu.VMEM((cb, 4 * ns), F32)],
        compiler_params=_params(("parallel",)),
        name="s5_scan",
    )(ut.reshape(cb, e, tl), lre[:, None, :], lim[:, None, :], ldt[:, None, :],
      lre[:, :, None], lim[:, :, None], ldt[:, :, None],
      bt_r, bt_i, btil_r, btil_i, crep_r, crep_i, ct_r, ct_i, dterm)

    return pl.pallas_call(
        _s5_out_kernel,
        grid=(bsz, ntile),
        in_specs=[pl.BlockSpec((nsub, 1, e, tl), lambda b, t: (t, b, 0, 0)), tok(e), tok(d), _mod_spec(nct),
                  _const_spec((e, 2 * e)), _const_spec((1, 2 * e)), _const_spec((e, d)),
                  _const_spec((1, d)), _const_spec((1, d))],
        out_specs=tok(d),
        out_shape=jax.ShapeDtypeStruct((bsz, nt, d), F32),
        compiler_params=_params(("parallel", "parallel")),
        name="s5_out",
    )(yt.reshape(nch, bsz, e, tl), g, xt, mod, w_glu.astype(BF16), b_glu.reshape(1, -1), w_out.astype(BF16),
      ln_g.reshape(1, d), ln_b.reshape(1, d))


def _conv_in_kernel(x_ref, mod_ref, w_ref, v_ref, g_ref):
    h = _modulate(x_ref[0], mod_ref).astype(BF16)
    z = jnp.dot(h, w_ref[...], preferred_element_type=F32)
    v_ref[0] = (z[:, :E_BRANCH] * _sigmoid(z[:, E_BRANCH:2 * E_BRANCH])).astype(BF16)
    g_ref[0] = z[:, 2 * E_BRANCH:].astype(BF16)


def _conv_out_kernel(v_ref, vp_ref, vn_ref, g_ref, x_ref, mod_ref, dw_ref, dwb_ref, cg_ref, cb_ref, wout_ref,
                     lng_ref, lnb_ref, out_ref, vext_sc, va_sc, vb_sc, y_sc, *, nct, ntile, row_block):
    tm = v_ref.shape[1]
    t = pl.program_id(1)
    has_prev = jnp.logical_and(t != 0, t != nct)
    has_next = jnp.logical_and(t != nct - 1, t != ntile - 1)
    vext_sc[0:CONV_HALO] = jnp.where(has_prev, vp_ref[0].astype(F32), 0.0)
    vext_sc[CONV_HALO:CONV_HALO + tm] = v_ref[0].astype(F32)
    vext_sc[CONV_HALO + tm:2 * CONV_HALO + tm] = jnp.where(has_next, vn_ref[0].astype(F32), 0.0)
    vext_sc[2 * CONV_HALO + tm:3 * CONV_HALO + tm] = jnp.zeros((CONV_HALO, vext_sc.shape[1]), F32)
    n_b = vb_sc.shape[0]
    va_sc[...] = vext_sc[...].astype(BF16)
    vb_sc[...] = vext_sc[SUBLANE:SUBLANE + n_b].astype(BF16)

    shift0 = CONV_HALO - CONV_PAD
    win = row_block + BF16_ROWS

    def lane_block(cbk, carry):
        lo = pl.multiple_of(cbk * LANE, LANE)
        bias = dwb_ref[:, pl.ds(lo, LANE)]
        for rb in range(tm // row_block):
            acc = jnp.zeros((row_block, LANE), F32)
            for rho in range(SUBLANE):
                z = None
                for k in range(CONV_WIDTH):
                    if (k + shift0) % SUBLANE != rho:
                        continue
                    start = rb * row_block + (k + shift0) - rho
                    if start % BF16_ROWS == 0:
                        x = va_sc[pl.ds(start, win), pl.ds(lo, LANE)]
                    else:
                        x = vb_sc[pl.ds(start - SUBLANE, win), pl.ds(lo, LANE)]
                    term = x * jnp.tile(dw_ref[k, :, pl.ds(lo, LANE)], (win // BF16_ROWS, 1))
                    z = term if z is None else z + term
                if z is not None:
                    acc = acc + z.astype(F32)[rho:rho + row_block]
            y_sc[rb * row_block:(rb + 1) * row_block, pl.ds(lo, LANE)] = acc + bias
        return carry
    lax.fori_loop(0, E_BRANCH // LANE, lane_block, 0)

    c = _silu(_layernorm(y_sc[...], cg_ref[...], cb_ref[...]))
    y = jnp.dot(c.astype(BF16) * _silu(g_ref[0]), wout_ref[...], preferred_element_type=F32)
    out_ref[0] = _residual_norm(x_ref[0], mod_ref, y, lng_ref, lnb_ref)


def _conv_layer(xt, mod, n_ctx, w_in, dw, dw_b, cln_g, cln_b, w_out, ln_g, ln_b):
    bsz, nt, d = xt.shape
    tm, e = TOK_TILE, E_BRANCH
    nct, ntile = n_ctx // tm, nt // tm
    assert n_ctx % tm == 0 and nt % tm == 0
    tok = lambda last: pl.BlockSpec((1, tm, last), lambda b, t: (b, t, 0))
    v, g = pl.pallas_call(
        _conv_in_kernel,
        grid=(bsz, ntile),
        in_specs=[tok(d), _mod_spec(nct), _const_spec((d, 3 * e))],
        out_specs=[tok(e), tok(e)],
        out_shape=[jax.ShapeDtypeStruct((bsz, nt, e), BF16), jax.ShapeDtypeStruct((bsz, nt, e), BF16)],
        compiler_params=_params(("parallel", "parallel")),
        name="conv_in",
    )(xt, mod, w_in.astype(BF16))

    hb = tm // CONV_HALO
    nhb = nt // CONV_HALO
    dw_tiles = jnp.broadcast_to(dw.astype(BF16)[:, None, :], (CONV_WIDTH, BF16_ROWS, e))
    return pl.pallas_call(
        functools.partial(_conv_out_kernel, nct=nct, ntile=ntile, row_block=128),
        grid=(bsz, ntile),
        in_specs=[tok(e),
                  pl.BlockSpec((1, CONV_HALO, e), lambda b, t: (b, jnp.maximum(t * hb - 1, 0), 0)),
                  pl.BlockSpec((1, CONV_HALO, e), lambda b, t: (b, jnp.minimum((t + 1) * hb, nhb - 1), 0)),
                  tok(e), tok(d), _mod_spec(nct),
                  _const_spec((CONV_WIDTH, BF16_ROWS, e)), _const_spec((1, e)), _const_spec((1, e)),
                  _const_spec((1, e)), _const_spec((e, d)), _const_spec((1, d)), _const_spec((1, d))],
        out_specs=tok(d),
        out_shape=jax.ShapeDtypeStruct((bsz, nt, d), F32),
        scratch_shapes=[pltpu.VMEM((tm + 3 * CONV_HALO, e), F32), pltpu.VMEM((tm + 3 * CONV_HALO, e), BF16),
                        pltpu.VMEM((tm + 2 * CONV_HALO, e), BF16), pltpu.VMEM((tm, e), F32)],
        compiler_params=_params(("parallel", "parallel")),
        name="conv_out",
    )(v, v, v, g, xt, mod, dw_tiles, dw_b.reshape(1, e), cln_g.reshape(1, e), cln_b.reshape(1, e),
      w_out.astype(BF16), ln_g.reshape(1, d), ln_b.reshape(1, d))


def kernel(x, c, ctx, c_ctx, w_mod, b_mod, ln_g, ln_b, mla_w_in, mla_q_norm, mla_kv_norm, mla_w_uq, mla_w_uk, mla_w_uv, mla_w_out, s5_w_in, s5_lam_re, s5_lam_im, s5_log_dt, s5_b_re, s5_b_im, s5_c_re, s5_c_im, s5_d, s5_w_glu, s5_b_glu, s5_w_out, cv_w_in, cv_dw, cv_dw_b, cv_ln_g, cv_ln_b, cv_w_out):
    n_ctx, n_lat = ctx.shape[1], x.shape[1]
    mod = _modulation(c, c_ctx, w_mod, b_mod)
    tables = _rope_tables(n_ctx, n_lat)
    xt = None
    for i in range(DEPTH):
        kind, j = i % N_MIXERS, i // N_MIXERS
        if kind == 0:
            stream = (ctx, x) if i == 0 else (xt,)
            xt = _mla_layer(stream, mod[i], tables, n_ctx, mla_w_in[j], mla_q_norm[j], mla_kv_norm[j], mla_w_uq[j],
                            mla_w_uk[j], mla_w_uv[j], mla_w_out[j], ln_g[i], ln_b[i], i < DEPTH - 1)
        elif kind == 1:
            xt = _s5_layer(xt, mod[i], n_ctx, s5_w_in[j], s5_lam_re[j], s5_lam_im[j], s5_log_dt[j], s5_b_re[j],
                           s5_b_im[j], s5_c_re[j], s5_c_im[j], s5_d[j], s5_w_glu[j], s5_b_glu[j], s5_w_out[j],
                           ln_g[i], ln_b[i])
        else:
            xt = _conv_layer(xt, mod[i], n_ctx, cv_w_in[j], cv_dw[j], cv_dw_b[j], cv_ln_g[j], cv_ln_b[j],
                             cv_w_out[j], ln_g[i], ln_b[i])
    return xt
```

```python
import functools
import math

import jax
import jax.numpy as jnp
from jax import lax
from jax.experimental import pallas as pl
from jax.experimental.pallas import tpu as pltpu

F32 = jnp.float32
BF16 = jnp.bfloat16

D_MODEL = 1024
DEPTH = 4
N_MIXERS = 3
GRID_W = 64
E_BRANCH = 2 * D_MODEL
MLA_HEADS = 16
MLA_NOPE = 128
MLA_ROPE = 64
MLA_V = 128
MLA_Q_RANK = 256
MLA_KV_RANK = 128
MLA_KEY = 256
MLA_QCOLS = 384
ROPE_BASE = 10000.0
S5_GROUP = 16
S5_GROUPS = E_BRANCH // S5_GROUP
S5_STATE = 64
S5_CHUNK = 128
CONV_WIDTH = 31
CONV_PAD = CONV_WIDTH // 2
CONV_HALO = 16
DEEPNORM_ALPHA = (2.0 * DEPTH) ** 0.25
NORM_EPS = 1e-6
LANE = 128
SUBLANE = 8
VMEM_LIMIT = 56 * 1024 * 1024

TOK_TILE = 256
ATT_TQ = 128
ATT_TK = 512
ATT_ROW_BLOCK = MLA_HEADS * ATT_TQ
MOD_ROWS = 24


def _sigmoid(x):
    return 1.0 / (1.0 + jnp.exp(-x))


def _silu(x):
    return x * _sigmoid(x)


def _gelu_tanh(x):
    return 0.5 * x * (1.0 + jnp.tanh(math.sqrt(2.0 / math.pi) * (x + 0.044715 * (x * x * x))))


def _layernorm(r, g, b):
    mu = jnp.mean(r, axis=-1, keepdims=True)
    d = r - mu
    var = jnp.mean(d * d, axis=-1, keepdims=True)
    return d * lax.rsqrt(var + NORM_EPS) * g + b


def _rmsnorm(x, g):
    return x * lax.rsqrt(jnp.mean(x * x, axis=-1, keepdims=True) + NORM_EPS) * g


def _params(sem):
    return pltpu.CompilerParams(dimension_semantics=sem, vmem_limit_bytes=VMEM_LIMIT)


def _const_spec(shape):
    nd = len(shape)
    return pl.BlockSpec(shape, lambda *_: (0,) * nd, pipeline_mode=pl.Buffered(1))


def _mod_kernel(c_ref, w_ref, b_ref, o_ref):
    c = c_ref[...]
    o_ref[0] = jnp.dot(_silu(c), w_ref[0], preferred_element_type=F32,
                       precision=lax.Precision.HIGHEST) + b_ref[0]


def _modulation(c, c_ctx, w_mod, b_mod):
    bsz, d = c.shape
    assert bsz + 1 <= MOD_ROWS
    rows = jnp.concatenate([c, c_ctx[None], jnp.zeros((MOD_ROWS - bsz - 1, d), F32)], axis=0)
    out = pl.pallas_call(
        _mod_kernel,
        grid=(DEPTH, 3),
        in_specs=[pl.BlockSpec((MOD_ROWS, d), lambda i, j: (0, 0)),
                  pl.BlockSpec((1, d, d), lambda i, j: (i, 0, j)),
                  pl.BlockSpec((1, 1, d), lambda i, j: (i, 0, j))],
        out_specs=pl.BlockSpec((1, MOD_ROWS, d), lambda i, j: (i, 0, j)),
        out_shape=jax.ShapeDtypeStruct((DEPTH, MOD_ROWS, 3 * d), F32),
        compiler_params=_params(("parallel", "parallel")),
        name="modulation",
    )(rows, w_mod, b_mod.reshape(DEPTH, 1, 3 * d))
    out = out.reshape(DEPTH, MOD_ROWS, 3, d)
    lat = out[:, :bsz]
    ctx = jnp.broadcast_to(out[:, bsz][:, None], lat.shape)
    return jnp.stack([ctx, lat], axis=2)


def _mod_spec(nct):
    return pl.BlockSpec((1, 1, 3, D_MODEL), lambda b, t: (b, jnp.where(t < nct, 0, 1), 0, 0))


def _stream_specs(stream, tm, nct, t_off=0):
    d = stream[0].shape[-1]
    if len(stream) == 1:
        return [pl.BlockSpec((1, tm, d), lambda b, t: (b, t + t_off, 0))]
    return [pl.BlockSpec((1, tm, d), lambda b, t: (b, jnp.minimum(t + t_off, nct - 1), 0)),
            pl.BlockSpec((1, tm, d), lambda b, t: (b, jnp.maximum(t + t_off - nct, 0), 0))]


def _stream_tile(x_refs, nct, t_off=0):
    if len(x_refs) == 1:
        return x_refs[0][0]
    return jnp.where(pl.program_id(1) + t_off < nct, x_refs[0][0], x_refs[1][0])


def _modulate(x, mod_ref):
    return x * (1.0 + mod_ref[0, 0, 1:2, :]) + mod_ref[0, 0, 0:1, :]


def _residual_norm(x, mod_ref, y, lng_ref, lnb_ref):
    r = DEEPNORM_ALPHA * x + mod_ref[0, 0, 2:3, :] * y
    return _layernorm(r, lng_ref[...], lnb_ref[...])


def _rot_cols(w):
    x1, x2, x3, x4 = jnp.split(w, 4, axis=-1)
    return jnp.concatenate([-x2, x1, -x4, x3], axis=-1)


def _fold_kernel(uq_ref, uk_ref, o_ref):
    o_ref[0] = jnp.dot(uq_ref[0], uk_ref[0], preferred_element_type=F32, precision=lax.Precision.HIGHEST)


def _mla_in_kernel(*refs, n_x, nct, tq, qscale):
    x_refs = refs[:n_x]
    mod_ref, cq_ref, sq_ref, ck_ref, w1_ref, w2_ref, qn_ref, kvn_ref, q_ref, k_ref, v_ref, g_ref = refs[n_x:]
    tm = x_refs[0].shape[1]
    h = _modulate(_stream_tile(x_refs, nct), mod_ref).astype(BF16)
    z = jnp.dot(h, w1_ref[...], preferred_element_type=F32)
    qn = _rmsnorm(z[:, 0:MLA_Q_RANK], qn_ref[...]).astype(BF16)
    ckv = _rmsnorm(z[:, 256:384], kvn_ref[...]).astype(BF16)
    sq = sq_ref[...]
    kk = z[:, 384:512] * ck_ref[...] + z[:, 512:640] * sq
    k_ref[0, :, 0:LANE] = ckv
    k_ref[0, :, LANE:2 * LANE] = kk.astype(BF16)
    v_ref[0, :, 0:LANE] = ckv
    lane = lax.broadcasted_iota(jnp.int32, (tm, LANE), 1)
    v_ref[0, :, LANE:2 * LANE] = jnp.where(lane == 0, 1.0, 0.0).astype(BF16)
    g_ref[0] = z[:, 640:].astype(BF16)
    qq = jnp.dot(qn, w2_ref[...], preferred_element_type=F32)
    cq = cq_ref[...]
    for hh in range(MLA_HEADS):
        base = hh * MLA_QCOLS
        qa = qq[:, base:base + LANE] * qscale
        qr = (qq[:, base + LANE:base + 2 * LANE] * cq + qq[:, base + 2 * LANE:base + 3 * LANE] * sq) * qscale
        for s in range(tm // tq):
            q_ref[0, s, hh * tq:(hh + 1) * tq, 0:LANE] = qa[s * tq:(s + 1) * tq].astype(BF16)
            q_ref[0, s, hh * tq:(hh + 1) * tq, LANE:2 * LANE] = qr[s * tq:(s + 1) * tq].astype(BF16)


def _attn_kernel(q_ref, k_ref, v_ref, o_ref, m_sc, acc_sc, sa_sc, sb_sc, *, tq, tk, q_off, nct_q, n_ctx, n_lat_chunks):
    q = q_ref[0, 0]

    rows = q.shape[0]

    def scores(buf, off, size):
        kc = k_ref[0, pl.ds(off, size), :]
        buf[:, 0:size] = lax.dot_general(q, kc, (((1,), (1,)), ((), ())), preferred_element_type=F32)

    def update(buf, off, size, first=False):
        vc = v_ref[0, pl.ds(off, size), :]
        for r0 in range(0, rows, ATT_ROW_BLOCK):
            rs = slice(r0, r0 + ATT_ROW_BLOCK)
            s = buf[rs, 0:size]
            m_cur = jnp.max(s, axis=-1, keepdims=True)
            m_new = jnp.broadcast_to(m_cur, (ATT_ROW_BLOCK, LANE)) if first else jnp.maximum(m_sc[rs], m_cur)
            p = jnp.concatenate([jnp.exp2(s[:, j * LANE:(j + 1) * LANE] - m_new).astype(BF16)
                                 for j in range(size // LANE)], axis=1)
            pv = jnp.dot(p, vc, preferred_element_type=F32)
            if first:
                acc_sc[rs] = pv
            else:
                alpha = jnp.exp2(m_sc[rs] - m_new)
                acc_sc[rs, 0:LANE] = alpha * acc_sc[rs, 0:LANE] + pv[:, 0:LANE]
                acc_sc[rs, LANE:2 * LANE] = alpha * acc_sc[rs, LANE:2 * LANE] + pv[:, LANE:2 * LANE]
            m_sc[rs] = m_new

    def lat_off(c):
        return pl.multiple_of(n_ctx + c * tk, LANE)

    def latent_chunks():
        n = n_lat_chunks
        scores(sa_sc, n_ctx, tk)
        if n == 1:
            update(sa_sc, n_ctx, tk)
            return

        def pair(i, carry):
            scores(sb_sc, lat_off(2 * i + 1), tk)
            update(sa_sc, lat_off(2 * i), tk)
            scores(sa_sc, lat_off(2 * i + 2), tk)
            update(sb_sc, lat_off(2 * i + 1), tk)
            return carry

        npairs = n // 2
        if npairs > 1:
            lax.fori_loop(0, npairs - 1, pair, 0)
        last = 2 * (npairs - 1)
        scores(sb_sc, n_ctx + (last + 1) * tk, tk)
        update(sa_sc, n_ctx + last * tk, tk)
        if n % 2:
            scores(sa_sc, n_ctx + (last + 2) * tk, tk)
        update(sb_sc, n_ctx + (last + 1) * tk, tk)
        if n % 2:
            update(sa_sc, n_ctx + (last + 2) * tk, tk)

    scores(sb_sc, 0, n_ctx)
    update(sb_sc, 0, n_ctx, first=True)
    if q_off >= nct_q:
        latent_chunks()
    else:
        pl.when(pl.program_id(1) + q_off >= nct_q)(latent_chunks)

    o = acc_sc[:, 0:LANE] * (1.0 / acc_sc[:, LANE:LANE + 1])
    for hh in range(MLA_HEADS):
        o_ref[0, :, hh * LANE:(hh + 1) * LANE] = o[hh * tq:(hh + 1) * tq].astype(BF16)


def _mla_out_kernel(*refs, n_x, nct, t_off):
    x_refs = refs[:n_x]
    o_ref, g_ref, mod_ref, wuv_ref, wout_ref, lng_ref, lnb_ref, out_ref = refs[n_x:]
    parts = []
    for j in range(MLA_HEADS // 2):
        cols = slice(j * 2 * LANE, (j + 1) * 2 * LANE)
        v = jnp.dot(o_ref[0, :, cols], wuv_ref[j], preferred_element_type=F32)
        parts.append(v.astype(BF16) * _silu(g_ref[0, :, cols]))
    vg = jnp.concatenate(parts, axis=1)
    y = jnp.dot(vg, wout_ref[...], preferred_element_type=F32)
    out_ref[0] = _residual_norm(_stream_tile(x_refs, nct, t_off), mod_ref, y, lng_ref, lnb_ref)


def _mla_layer(stream, mod, tables, n_ctx, w_in, q_norm, kv_norm, w_uq, w_uk, w_uv, w_out, ln_g, ln_b, with_ctx_out):
    bsz, _, d = stream[-1].shape
    nt = sum(a.shape[1] for a in stream)
    assert len(stream) == 1 or with_ctx_out
    tm, tq = TOK_TILE, ATT_TQ
    tk = ATT_TK
    assert n_ctx % tm == 0 and nt % tm == 0 and n_ctx <= ATT_TK and (nt - n_ctx) % tk == 0
    nct = n_ctx // tm
    ntile = nt // tm
    hdim = MLA_HEADS
    qscale = (MLA_NOPE + MLA_ROPE) ** -0.5 * math.log2(math.e)

    kr = w_in[:, 384:448]
    w1 = jnp.concatenate([w_in[:, 0:384], kr, kr, _rot_cols(kr), jnp.zeros_like(kr), w_in[:, 448:]],
                         axis=1).astype(BF16)
    uq = w_uq.reshape(MLA_Q_RANK, hdim, MLA_NOPE + MLA_ROPE)
    fold = pl.pallas_call(
        _fold_kernel,
        grid=(hdim,),
        in_specs=[pl.BlockSpec((1, MLA_Q_RANK, MLA_NOPE), lambda h: (h, 0, 0)),
                  pl.BlockSpec((1, MLA_NOPE, MLA_KV_RANK), lambda h: (h, 0, 0))],
        out_specs=pl.BlockSpec((1, MLA_Q_RANK, MLA_KV_RANK), lambda h: (h, 0, 0)),
        out_shape=jax.ShapeDtypeStruct((hdim, MLA_Q_RANK, MLA_KV_RANK), F32),
        compiler_params=_params(("parallel",)),
        name="mla_fold",
    )(jnp.transpose(uq[:, :, :MLA_NOPE], (1, 0, 2)), jnp.transpose(w_uk, (1, 2, 0)))
    rp = uq[:, :, MLA_NOPE:]
    w2 = jnp.concatenate([jnp.transpose(fold, (1, 0, 2)), rp, rp, _rot_cols(rp), jnp.zeros_like(rp)],
                         axis=2).reshape(MLA_Q_RANK, hdim * MLA_QCOLS).astype(BF16)
    wuv = jnp.transpose(w_uv, (1, 0, 2))
    zero = jnp.zeros_like(wuv[0::2])
    wuv2 = jnp.concatenate([jnp.concatenate([wuv[0::2], zero], axis=2),
                            jnp.concatenate([zero, wuv[1::2]], axis=2)], axis=1).astype(BF16)
    cq, sq, ck = tables

    tok = lambda last: pl.BlockSpec((1, tm, last), lambda b, t: (b, t, 0))
    tab = pl.BlockSpec((tm, LANE), lambda b, t: (t, 0))
    q, k, v, g = pl.pallas_call(
        functools.partial(_mla_in_kernel, n_x=len(stream), nct=nct, tq=tq, qscale=qscale),
        grid=(bsz, ntile),
        in_specs=_stream_specs(stream, tm, nct) + [_mod_spec(nct), tab, tab, tab,
                  _const_spec(w1.shape), _const_spec(w2.shape),
                  _const_spec((1, MLA_Q_RANK)), _const_spec((1, MLA_KV_RANK))],
        out_specs=[pl.BlockSpec((1, tm // tq, hdim * tq, MLA_KEY), lambda b, t: (b, t, 0, 0)),
                   tok(MLA_KEY), tok(MLA_KEY), tok(hdim * MLA_V)],
        out_shape=[jax.ShapeDtypeStruct((bsz, nt // tq, hdim * tq, MLA_KEY), BF16),
                   jax.ShapeDtypeStruct((bsz, nt, MLA_KEY), BF16),
                   jax.ShapeDtypeStruct((bsz, nt, MLA_KEY), BF16),
                   jax.ShapeDtypeStruct((bsz, nt, hdim * MLA_V), BF16)],
        compiler_params=_params(("parallel", "parallel")),
        name="mla_in",
    )(*stream, mod, cq, sq, ck, w1, w2, q_norm.reshape(1, -1), kv_norm.reshape(1, -1))

    q_off = 0 if with_ctx_out else n_ctx // tq
    nq = nt // tq - q_off
    rows = hdim * tq
    o = pl.pallas_call(
        functools.partial(_attn_kernel, tq=tq, tk=tk, q_off=q_off, nct_q=n_ctx // tq, n_ctx=n_ctx,
                          n_lat_chunks=(nt - n_ctx) // tk),
        grid=(bsz, nq),
        in_specs=[pl.BlockSpec((1, 1, rows, MLA_KEY), lambda b, t: (b, t + q_off, 0, 0)),
                  pl.BlockSpec((1, nt, MLA_KEY), lambda b, t: (b, 0, 0)),
                  pl.BlockSpec((1, nt, MLA_KEY), lambda b, t: (b, 0, 0))],
        out_specs=pl.BlockSpec((1, tq, hdim * MLA_V), lambda b, t: (b, t, 0)),
        out_shape=jax.ShapeDtypeStruct((bsz, nq * tq, hdim * MLA_V), BF16),
        scratch_shapes=[pltpu.VMEM((rows, LANE), F32), pltpu.VMEM((rows, 2 * LANE), F32),
                        pltpu.VMEM((rows, tk), F32), pltpu.VMEM((rows, tk), F32)],
        compiler_params=_params(("parallel", "parallel")),
        name="mla_attention",
    )(q, k, v)

    t_off = 0 if with_ctx_out else nct
    n_out = ntile - t_off
    sh = lambda last: pl.BlockSpec((1, tm, last), lambda b, t: (b, t + t_off, 0))
    return pl.pallas_call(
        functools.partial(_mla_out_kernel, n_x=len(stream), nct=nct, t_off=t_off),
        grid=(bsz, n_out),
        in_specs=_stream_specs(stream, tm, nct, t_off) + [
                  tok(hdim * MLA_V), sh(hdim * MLA_V),
                  pl.BlockSpec((1, 1, 3, d), lambda b, t: (b, jnp.where(t + t_off < nct, 0, 1), 0, 0)),
                  _const_spec(wuv2.shape), _const_spec((hdim * MLA_V, d)),
                  _const_spec((1, d)), _const_spec((1, d))],
        out_specs=tok(d),
        out_shape=jax.ShapeDtypeStruct((bsz, n_out * tm, d), F32),
        compiler_params=_params(("parallel", "parallel")),
        name="mla_out",
    )(*stream, o, g, mod, wuv2, w_out.astype(BF16), ln_g.reshape(1, d), ln_b.reshape(1, d))


def _rope_tables(n_ctx, n_lat):
    rows = n_lat // GRID_W
    r, col = jnp.meshgrid(jnp.arange(rows, dtype=F32), jnp.arange(GRID_W, dtype=F32), indexing="ij")
    quarter = MLA_ROPE // 4
    inv = ROPE_BASE ** (-jnp.arange(quarter, dtype=F32) / quarter)
    ang_r = r.reshape(-1)[:, None] * inv
    ang_c = col.reshape(-1)[:, None] * inv
    ang = jnp.concatenate([ang_r, ang_r, ang_c, ang_c], axis=-1)
    cos, sin = jnp.cos(ang), jnp.sin(ang)
    one, zero = jnp.ones_like(cos), jnp.zeros_like(cos)
    czero, cone = jnp.zeros((n_ctx, MLA_ROPE), F32), jnp.ones((n_ctx, MLA_ROPE), F32)
    cq = jnp.concatenate([jnp.concatenate([czero, cone], 1), jnp.concatenate([cos, one], 1)], 0)
    sq = jnp.concatenate([jnp.concatenate([czero, czero], 1), jnp.concatenate([sin, zero], 1)], 0)
    ck = jnp.concatenate([jnp.concatenate([czero, cone], 1), jnp.concatenate([cos, zero], 1)], 0)
    return cq, sq, ck


def _s5_in_kernel(x_ref, mod_ref, wut_ref, wg_ref, ut_ref, g_ref):
    h = _modulate(x_ref[0], mod_ref).astype(BF16)
    g_ref[0] = jnp.dot(h, wg_ref[...], preferred_element_type=F32).astype(BF16)
    ut = lax.dot_general(wut_ref[...], h, (((1,), (1,)), ((), ())), preferred_element_type=F32)
    for s in range(ut_ref.shape[0]):
        ut_ref[s, 0] = ut[:, s * S5_CHUNK:(s + 1) * S5_CHUNK]


def _cexp(scale, ar, ai):
    mag = jnp.exp(scale * ar)
    ang = scale * ai
    return mag * jnp.cos(ang), mag * jnp.sin(ang)


def _s5_scan_kernel(u_ref, lre_r, lim_r, ldt_r, lre_c, lim_c, ldt_c, btr_ref, bti_ref, btilr_ref, btili_ref,
                    crepr_ref, crepi_ref, ctr_ref, cti_ref, dterm_ref, y_ref,
                    lhs_sc, wst_sc, wout_sc, wtoep_sc, v_sc, s_sc, hf_sc, hb_sc, *, nb, order_bwd):
    t_len = S5_CHUNK
    half = S5_STATE
    hi = lax.Precision.HIGHEST
    n_chunks = len(order_bwd)

    ar = jnp.minimum(lre_r[0], -1e-4) * jnp.exp(ldt_r[0])
    ai = lim_r[0] * jnp.exp(ldt_r[0])
    lr, li = jnp.minimum(lre_r[0], -1e-4), lim_r[0]
    lbr, lbi = _cexp(1.0, ar, ai)
    den = lr * lr + li * li
    fr = ((lbr - 1.0) * lr + lbi * li) / den
    fi = (lbi * lr - (lbr - 1.0) * li) / den
    tr, ti = _cexp(float(t_len), ar, ai)

    row = lax.broadcasted_iota(jnp.int32, (t_len, LANE), 0)
    lane = lax.broadcasted_iota(jnp.int32, (t_len, LANE), 1)
    e_s = jnp.where(lane < half, t_len - 1 - row, row).astype(F32)
    es_r, es_i = _cexp(e_s, ar, ai)
    bbr = btr_ref[0] * fr - bti_ref[0] * fi
    bbi = btr_ref[0] * fi + bti_ref[0] * fr
    for i in range(S5_GROUP):
        br, bi = bbr[i:i + 1], bbi[i:i + 1]
        wst_sc[i * t_len:(i + 1) * t_len, 0:LANE] = (es_r * br - es_i * bi).astype(BF16)
        wst_sc[i * t_len:(i + 1) * t_len, LANE:2 * LANE] = (es_r * bi + es_i * br).astype(BF16)

    dt_c = jnp.exp(ldt_c[0])
    ar_c = jnp.minimum(lre_c[0], -1e-4) * dt_c
    ai_c = lim_c[0] * dt_c
    is_f = row < half
    e_k = jnp.where(is_f, lane, t_len - lane).astype(F32)
    pk_r, pk_i = _cexp(e_k, ar_c, ai_c)
    lbr_c, lbi_c = _cexp(1.0, ar_c, ai_c)
    po_r = jnp.where(is_f, pk_r * lbr_c - pk_i * lbi_c, pk_r)
    po_i = jnp.where(is_f, pk_r * lbi_c + pk_i * lbr_c, pk_i)
    ctr, cti = ctr_ref[0], cti_ref[0]
    for o in range(S5_GROUP):
        cr, ci = ctr[:, o:o + 1], cti[:, o:o + 1]
        wout_sc[0:LANE, o * t_len:(o + 1) * t_len] = (cr * po_r - ci * po_i).astype(BF16)
        wout_sc[LANE:2 * LANE, o * t_len:(o + 1) * t_len] = (-(cr * po_i + ci * po_r)).astype(BF16)

    mbr = btilr_ref[0] * fr - btili_ref[0] * fi
    mbi = btilr_ref[0] * fi + btili_ref[0] * fr
    m_r = crepr_ref[0] * mbr - crepi_ref[0] * mbi
    m_i = crepr_ref[0] * mbi + crepi_ref[0] * mbr
    zero = jnp.zeros_like(pk_r)
    pcat_r = jnp.concatenate([jnp.where(is_f, zero, pk_r), jnp.where(is_f, pk_r, zero)], axis=1)
    pcat_i = jnp.concatenate([jnp.where(is_f, zero, pk_i), jnp.where(is_f, pk_i, zero)], axis=1)
    v = (jnp.dot(m_r, pcat_r, preferred_element_type=F32, precision=hi)
         - jnp.dot(m_i, pcat_i, preferred_element_type=F32, precision=hi))
    lane_m = lax.broadcasted_iota(jnp.int32, m_r.shape, 1)
    kb0 = jnp.sum(jnp.where(lane_m >= half, m_r, 0.0), axis=1, keepdims=True)
    lane_v = lax.broadcasted_iota(jnp.int32, v.shape, 1)
    v_sc[...] = v + jnp.where(lane_v == t_len, kb0 + dterm_ref[0], 0.0)

    for o in range(S5_GROUP):
        vblk = v_sc[o * S5_GROUP:(o + 1) * S5_GROUP, :]
        for i in range(S5_GROUP):
            row_b = jnp.broadcast_to(vblk[i:i + 1], (t_len, 2 * t_len))
            rolled = pltpu.roll(row_b, 0, 1, stride=1, stride_axis=0)
            wtoep_sc[i * t_len:(i + 1) * t_len, o * t_len:(o + 1) * t_len] = rolled[:, t_len:].astype(BF16)

    u_t = jnp.swapaxes(u_ref[...], 0, 1)
    for i in range(S5_GROUP):
        lhs_sc[:, i * t_len:(i + 1) * t_len] = u_t[i].astype(BF16)
    lhs = lhs_sc[...]
    s_sc[...] = jnp.dot(lhs, wst_sc[...], preferred_element_type=F32)

    lane_h = lax.broadcasted_iota(jnp.int32, (nb, LANE), 1)
    f_lane = lane_h < half
    h_r = jnp.zeros((nb, LANE), F32)
    h_i = jnp.zeros((nb, LANE), F32)
    for j in range(n_chunks):
        cf, cb = j, order_bwd[j]
        hf_sc[cf * nb:(cf + 1) * nb, 0:LANE] = h_r
        hf_sc[cf * nb:(cf + 1) * nb, LANE:2 * LANE] = h_i
        hb_sc[cb * nb:(cb + 1) * nb, 0:LANE] = h_r
        hb_sc[cb * nb:(cb + 1) * nb, LANE:2 * LANE] = h_i
        s_r = jnp.where(f_lane, s_sc[cf * nb:(cf + 1) * nb, 0:LANE], s_sc[cb * nb:(cb + 1) * nb, 0:LANE])
        s_i = jnp.where(f_lane, s_sc[cf * nb:(cf + 1) * nb, LANE:2 * LANE],
                        s_sc[cb * nb:(cb + 1) * nb, LANE:2 * LANE])
        h_r, h_i = tr * h_r - ti * h_i + s_r, tr * h_i + ti * h_r + s_i
    lane_2 = lax.broadcasted_iota(jnp.int32, hf_sc.shape, 1)
    h_in = jnp.where((lane_2 % LANE) < half, hf_sc[...], hb_sc[...])
    h_hi = h_in.astype(BF16)
    h_lo = (h_in - h_hi.astype(F32)).astype(BF16)
    wout = wout_sc[...]
    y = (jnp.dot(lhs, wtoep_sc[...], preferred_element_type=F32)
         + jnp.dot(h_hi, wout, preferred_element_type=F32)
         + jnp.dot(h_lo, wout, preferred_element_type=F32))
    y_t = jnp.stack([y[:, o * t_len:(o + 1) * t_len] for o in range(S5_GROUP)], axis=0)
    y_ref[...] = jnp.swapaxes(y_t, 0, 1)


def _s5_out_kernel(yt_ref, g_ref, x_ref, mod_ref, wglu_ref, bglu_ref, wout_ref, lng_ref, lnb_ref, out_ref):
    parts = [jnp.transpose(_gelu_tanh(yt_ref[s, 0])).astype(BF16) for s in range(yt_ref.shape[0])]
    a = jnp.concatenate(parts, axis=0)
    z = jnp.dot(a, wglu_ref[...], preferred_element_type=F32) + bglu_ref[...]
    t = (z[:, :E_BRANCH] * _sigmoid(z[:, E_BRANCH:])).astype(BF16) * _silu(g_ref[0])
    y = jnp.dot(t, wout_ref[...], preferred_element_type=F32)
    out_ref[0] = _residual_norm(x_ref[0], mod_ref, y, lng_ref, lnb_ref)


def _s5_layer(xt, mod, n_ctx, w_in, lam_re, lam_im, log_dt, b_re, b_im, c_re, c_im, d_skip, w_glu, b_glu, w_out,
              ln_g, ln_b):
    bsz, nt, d = xt.shape
    tm = TOK_TILE
    e, ng, gs, ns, tl = E_BRANCH, S5_GROUPS, S5_GROUP, S5_STATE, S5_CHUNK
    nct, ntile = n_ctx // tm, nt // tm
    nsub = tm // tl
    nch, nch_ctx = nt // tl, n_ctx // tl
    cb = nch * bsz
    assert n_ctx % tm == 0 and nt % tm == 0 and cb % 16 == 0 and bsz % 8 == 0

    tok = lambda last: pl.BlockSpec((1, tm, last), lambda b, t: (b, t, 0))
    ut, g = pl.pallas_call(
        _s5_in_kernel,
        grid=(bsz, ntile),
        in_specs=[tok(d), _mod_spec(nct), _const_spec((e, d)), _const_spec((d, e))],
        out_specs=[pl.BlockSpec((nsub, 1, e, tl), lambda b, t: (t, b, 0, 0)), tok(e)],
        out_shape=[jax.ShapeDtypeStruct((nch, bsz, e, tl), F32), jax.ShapeDtypeStruct((bsz, nt, e), BF16)],
        compiler_params=_params(("parallel", "parallel")),
        name="s5_in",
    )(xt, mod, jnp.transpose(w_in[:, :e]).astype(BF16), w_in[:, e:].astype(BF16))

    both = lambda a: jnp.concatenate([a[0], a[1]], axis=-1)
    lre, lim = both(lam_re), both(lam_im)
    ldt = jnp.repeat(jnp.transpose(log_dt), ns, axis=1)
    bt_r, bt_i = (both(jnp.swapaxes(a, 2, 3)) for a in (b_re, b_im))
    cr_r, cr_i = both(c_re), both(c_im)
    btil_r, btil_i = (jnp.tile(a, (1, gs, 1)) for a in (bt_r, bt_i))
    crep_r, crep_i = (jnp.repeat(a, gs, axis=1) for a in (cr_r, cr_i))
    ct_r, ct_i = (jnp.swapaxes(a, 1, 2) for a in (cr_r, cr_i))
    dterm = (d_skip.reshape(ng, gs, 1) * jnp.eye(gs, dtype=F32)).reshape(ng, gs * gs, 1)

    order_bwd = tuple(range(nch_ctx - 1, -1, -1)) + tuple(range(nch - 1, nch_ctx - 1, -1))
    row_spec = pl.BlockSpec((1, 1, 2 * ns), lambda gi: (gi, 0, 0))
    col_spec = pl.BlockSpec((1, 2 * ns, 1), lambda gi: (gi, 0, 0))
    g3 = lambda r, c: pl.BlockSpec((1, r, c), lambda gi: (gi, 0, 0))
    data_spec = pl.BlockSpec((cb, gs, tl), lambda gi: (0, gi, 0))
    yt = pl.pallas_call(
        functools.partial(_s5_scan_kernel, nb=bsz, order_bwd=order_bwd),
        grid=(ng,),
        in_specs=[data_spec, row_spec, row_spec, row_spec, col_spec, col_spec, col_spec,
                  g3(gs, 2 * ns), g3(gs, 2 * ns), g3(gs * gs, 2 * ns), g3(gs * gs, 2 * ns),
                  g3(gs * gs, 2 * ns), g3(gs * gs, 2 * ns), g3(2 * ns, gs), g3(2 * ns, gs), g3(gs * gs, 1)],
        out_specs=data_spec,
        out_shape=jax.ShapeDtypeStruct((cb, e, tl), F32),
        scratch_shapes=[pltpu.VMEM((cb, gs * tl), BF16), pltpu.VMEM((gs * tl, 4 * ns), BF16),
                        pltpu.VMEM((4 * ns, gs * tl), BF16), pltpu.VMEM((gs * tl, gs * tl), BF16),
                        pltpu.VMEM((gs * gs, 2 * tl), F32), pltpu.VMEM((cb, 4 * ns), F32),
                        pltpu.VMEM((cb, 4 * ns), F32), pltpu.VMEM((cb, 4 * ns), F32)],
        compiler_params=_params(("parallel",)),
        name="s5_scan",
    )(ut.reshape(cb, e, tl), lre[:, None, :], lim[:, None, :], ldt[:, None, :],
      lre[:, :, None], lim[:, :, None], ldt[:, :, None],
      bt_r, bt_i, btil_r, btil_i, crep_r, crep_i, ct_r, ct_i, dterm)

    return pl.pallas_call(
        _s5_out_kernel,
        grid=(bsz, ntile),
        in_specs=[pl.BlockSpec((nsub, 1, e, tl), lambda b, t: (t, b, 0, 0)), tok(e), tok(d), _mod_spec(nct),
                  _const_spec((e, 2 * e)), _const_spec((1, 2 * e)), _const_spec((e, d)),
                  _const_spec((1, d)), _const_spec((1, d))],
        out_specs=tok(d),
        out_shape=jax.ShapeDtypeStruct((bsz, nt, d), F32),
        compiler_params=_params(("parallel", "parallel")),
        name="s5_out",
    )(yt.reshape(nch, bsz, e, tl), g, xt, mod, w_glu.astype(BF16), b_glu.reshape(1, -1), w_out.astype(BF16),
      ln_g.reshape(1, d), ln_b.reshape(1, d))


def _conv_in_kernel(x_ref, mod_ref, w_ref, v_ref, g_ref):
    h = _modulate(x_ref[0], mod_ref).astype(BF16)
    z = jnp.dot(h, w_ref[...], preferred_element_type=F32)
    v_ref[0] = (z[:, :E_BRANCH] * _sigmoid(z[:, E_BRANCH:2 * E_BRANCH])).astype(BF16)
    g_ref[0] = z[:, 2 * E_BRANCH:].astype(BF16)


def _conv_out_kernel(v_ref, vp_ref, vn_ref, g_ref, x_ref, mod_ref, dw_ref, dwb_ref, cg_ref, cb_ref, wout_ref,
                     lng_ref, lnb_ref, out_ref, vext_sc, y_sc, *, nct, ntile, row_block):
    tm = v_ref.shape[1]
    t = pl.program_id(1)
    has_prev = jnp.logical_and(t != 0, t != nct)
    has_next = jnp.logical_and(t != nct - 1, t != ntile - 1)
    vext_sc[0:CONV_HALO] = jnp.where(has_prev, vp_ref[0].astype(F32), 0.0)
    vext_sc[CONV_HALO:CONV_HALO + tm] = v_ref[0].astype(F32)
    vext_sc[CONV_HALO + tm:2 * CONV_HALO + tm] = jnp.where(has_next, vn_ref[0].astype(F32), 0.0)

    shift0 = CONV_HALO - CONV_PAD

    def lane_block(cbk, carry):
        lo = pl.multiple_of(cbk * LANE, LANE)
        w = dw_ref[:, pl.ds(lo, LANE)]
        bias = dwb_ref[:, pl.ds(lo, LANE)]
        for rb in range(tm // row_block):
            acc = jnp.zeros((row_block, LANE), F32)
            for rho in range(SUBLANE):
                z = None
                for k in range(CONV_WIDTH):
                    if (k + shift0) % SUBLANE != rho:
                        continue
                    start = rb * row_block + (k + shift0) - rho
                    term = vext_sc[pl.ds(start, row_block + SUBLANE), pl.ds(lo, LANE)] * w[k:k + 1]
                    z = term if z is None else z + term
                if z is not None:
                    acc = acc + z[rho:rho + row_block]
            y_sc[rb * row_block:(rb + 1) * row_block, pl.ds(lo, LANE)] = acc + bias
        return carry
    lax.fori_loop(0, E_BRANCH // LANE, lane_block, 0)

    c = _silu(_layernorm(y_sc[...], cg_ref[...], cb_ref[...]))
    y = jnp.dot(c.astype(BF16) * _silu(g_ref[0]), wout_ref[...], preferred_element_type=F32)
    out_ref[0] = _residual_norm(x_ref[0], mod_ref, y, lng_ref, lnb_ref)


def _conv_layer(xt, mod, n_ctx, w_in, dw, dw_b, cln_g, cln_b, w_out, ln_g, ln_b):
    bsz, nt, d = xt.shape
    tm, e = TOK_TILE, E_BRANCH
    nct, ntile = n_ctx // tm, nt // tm
    assert n_ctx % tm == 0 and nt % tm == 0
    tok = lambda last: pl.BlockSpec((1, tm, last), lambda b, t: (b, t, 0))
    v, g = pl.pallas_call(
        _conv_in_kernel,
        grid=(bsz, ntile),
        in_specs=[tok(d), _mod_spec(nct), _const_spec((d, 3 * e))],
        out_specs=[tok(e), tok(e)],
        out_shape=[jax.ShapeDtypeStruct((bsz, nt, e), BF16), jax.ShapeDtypeStruct((bsz, nt, e), BF16)],
        compiler_params=_params(("parallel", "parallel")),
        name="conv_in",
    )(xt, mod, w_in.astype(BF16))

    hb = tm // CONV_HALO
    nhb = nt // CONV_HALO
    dw_pad = jnp.concatenate([dw, jnp.zeros((32 - CONV_WIDTH, e), F32)], axis=0)
    return pl.pallas_call(
        functools.partial(_conv_out_kernel, nct=nct, ntile=ntile, row_block=128),
        grid=(bsz, ntile),
        in_specs=[tok(e),
                  pl.BlockSpec((1, CONV_HALO, e), lambda b, t: (b, jnp.maximum(t * hb - 1, 0), 0)),
                  pl.BlockSpec((1, CONV_HALO, e), lambda b, t: (b, jnp.minimum((t + 1) * hb, nhb - 1), 0)),
                  tok(e), tok(d), _mod_spec(nct),
                  _const_spec((32, e)), _const_spec((1, e)), _const_spec((1, e)), _const_spec((1, e)),
                  _const_spec((e, d)), _const_spec((1, d)), _const_spec((1, d))],
        out_specs=tok(d),
        out_shape=jax.ShapeDtypeStruct((bsz, nt, d), F32),
        scratch_shapes=[pltpu.VMEM((tm + 2 * CONV_HALO, e), F32), pltpu.VMEM((tm, e), F32)],
        compiler_params=_params(("parallel", "parallel")),
        name="conv_out",
    )(v, v, v, g, xt, mod, dw_pad, dw_b.reshape(1, e), cln_g.reshape(1, e), cln_b.reshape(1, e),
      w_out.astype(BF16), ln_g.reshape(1, d), ln_b.reshape(1, d))


def kernel(x, c, ctx, c_ctx, w_mod, b_mod, ln_g, ln_b, mla_w_in, mla_q_norm, mla_kv_norm, mla_w_uq, mla_w_uk, mla_w_uv, mla_w_out, s5_w_in, s5_lam_re, s5_lam_im, s5_log_dt, s5_b_re, s5_b_im, s5_c_re, s5_c_im, s5_d, s5_w_glu, s5_b_glu, s5_w_out, cv_w_in, cv_dw, cv_dw_b, cv_ln_g, cv_ln_b, cv_w_out):
    n_ctx, n_lat = ctx.shape[1], x.shape[1]
    mod = _modulation(c, c_ctx, w_mod, b_mod)
    tables = _rope_tables(n_ctx, n_lat)
    xt = None
    for i in range(DEPTH):
        kind, j = i % N_MIXERS, i // N_MIXERS
        if kind == 0:
            stream = (ctx, x) if i == 0 else (xt,)
            xt = _mla_layer(stream, mod[i], tables, n_ctx, mla_w_in[j], mla_q_norm[j], mla_kv_norm[j], mla_w_uq[j],
                            mla_w_uk[j], mla_w_uv[j], mla_w_out[j], ln_g[i], ln_b[i], i < DEPTH - 1)
        elif kind == 1:
            xt = _s5_layer(xt, mod[i], n_ctx, s5_w_in[j], s5_lam_re[j], s5_lam_im[j], s5_log_dt[j], s5_b_re[j],
                           s5_b_im[j], s5_c_re[j], s5_c_im[j], s5_d[j], s5_w_glu[j], s5_b_glu[j], s5_w_out[j],
                           ln_g[i], ln_b[i])
        else:
            xt = _conv_layer(xt, mod[i], n_ctx, cv_w_in[j], cv_dw[j], cv_dw_b[j], cv_ln_g[j], cv_ln_b[j],
                             cv_w_out[j], ln_g[i], ln_b[i])
    return xt
```
